```python
import math
import jax, jax.numpy as jnp
from jax import lax
import numpy as np

D_MODEL = 1024
BATCH = 2
SEQ = 16384
DEPTH = 4
DEC_BATCH = 8
DEC_SEQ = 32
PAST_LEN = 4096

CHUNK = 64
N_A_LAYERS = DEPTH // 2
N_B_LAYERS = DEPTH - N_A_LAYERS
GLA_HEADS = 4
GLA_DK = D_MODEL // 2
GLA_DV = D_MODEL
GLA_DK_H = GLA_DK // GLA_HEADS
GLA_DV_H = GLA_DV // GLA_HEADS
GLA_GATE_RANK = 16
GLA_GATE_TAU = 16.0
MLA_HEADS = 16
Q_LORA = 256
KV_LORA = 128
NOPE_DIM = 64
ROPE_DIM = 32
V_DIM = 64
ROPE_BASE = 10000.0
Q_BLOCK = 128
N_EXPERTS = 16
N_GROUPS = 4
EXPERTS_PER_GROUP = N_EXPERTS // N_GROUPS
TOP_K = 2
EXPERT_FF = 256
SHARED_FF = 256
EPS = 1e-6

kernel_name = 'hybrid_gla_mla_yoco_grouped_moe_stream'


def rms_norm(x, gain):
    xf = x.astype(jnp.float32)
    y = xf * lax.rsqrt(jnp.mean(xf * xf, axis=-1, keepdims=True) + EPS)
    return (y * gain.astype(jnp.float32)).astype(x.dtype)


def ada_rmsnorm(x, gain, shift, scale):
    return rms_norm(x, gain) * (1.0 + scale[:, None, :]) + shift[:, None, :]


def apply_rope(x, positions):
    half = ROPE_DIM // 2
    inv_freq = ROPE_BASE ** (-jnp.arange(half, dtype=jnp.float32) / half)
    ang = positions.astype(jnp.float32)[:, None] * inv_freq[None, :]
    bshape = (positions.shape[0],) + (1,) * (x.ndim - 3) + (half,)
    cos = jnp.cos(ang).reshape(bshape)
    sin = jnp.sin(ang).reshape(bshape)
    xf = x.astype(jnp.float32)
    x1, x2 = xf[..., :half], xf[..., half:]
    return jnp.concatenate([x1 * cos - x2 * sin, x2 * cos + x1 * sin], axis=-1).astype(x.dtype)


def gla_recurrence(q, k, v, log_alpha, s0):
    B, T, _ = q.shape
    n_chunks = -(-T // CHUNK)
    pad = n_chunks * CHUNK - T

    def to_chunks(t, dh):
        t = jnp.pad(t.astype(jnp.float32), ((0, 0), (0, pad), (0, 0)))
        return t.reshape(B, n_chunks, CHUNK, GLA_HEADS, dh).transpose(1, 0, 3, 2, 4)

    qc = to_chunks(q, GLA_DK_H)
    kc = to_chunks(k, GLA_DK_H)
    gc = to_chunks(log_alpha, GLA_DK_H)
    vc = to_chunks(v, GLA_DV_H)
    b = jnp.cumsum(gc, axis=3)
    b_last = b[:, :, :, -1:, :]
    q_dec = qc * jnp.exp(b)
    k_inv = kc * jnp.exp(-b)
    k_to_end = kc * jnp.exp(b_last - b)
    causal = jnp.tril(jnp.ones((CHUNK, CHUNK), dtype=bool))
    scores = jnp.where(causal, jnp.einsum('nbhcd,nbhsd->nbhcs', q_dec, k_inv), 0.0)
    o_intra = jnp.einsum('nbhcs,nbhsv->nbhcv', scores, vc)

    def step(s, xs):
        q_n, k_n, v_n, dec_n = xs
        o_inter = jnp.einsum('bhcd,bhdv->bhcv', q_n, s)
        s = dec_n[..., 0, :, None] * s + jnp.einsum('bhcd,bhcv->bhdv', k_n, v_n)
        return s, o_inter

    s_final, o_inter = lax.scan(step, s0.astype(jnp.float32), (q_dec, k_to_end, vc, jnp.exp(b_last)))
    o = (o_intra + o_inter).transpose(1, 0, 3, 2, 4).reshape(B, n_chunks * CHUNK, GLA_HEADS, GLA_DV_H)
    return o[:, :T], s_final


def gla_mixer(h, w_in, w_alpha2, b_alpha, out_gain, w_o, s0):
    B, T, _ = h.shape
    q, k, v, r, a_lr = jnp.split(
        h @ w_in, [GLA_DK, 2 * GLA_DK, 2 * GLA_DK + GLA_DV, 2 * GLA_DK + 2 * GLA_DV], axis=-1)
    log_alpha = jax.nn.log_sigmoid((a_lr @ w_alpha2 + b_alpha).astype(jnp.float32)) / GLA_GATE_TAU
    o, s_new = gla_recurrence(q * GLA_DK_H ** -0.5, k, v, log_alpha, s0)
    o = rms_norm(o.astype(h.dtype), out_gain).reshape(B, T, GLA_DV) * jax.nn.silu(r)
    return o @ w_o, s_new.astype(s0.dtype)


def mla_expand_kv(c_kv, w_ukv, k_nope_gain):
    B, L, _ = c_kv.shape
    kv = (c_kv @ w_ukv).reshape(B, L, MLA_HEADS, NOPE_DIM + V_DIM)
    k_nope, v = jnp.split(kv, [NOPE_DIM], axis=-1)
    return rms_norm(k_nope, k_nope_gain), v


def block_causal_attention(q_nope, q_rope, k_nope, k_rope, v, q_pos, k_pos):
    B, T, H, _ = q_nope.shape
    qb = Q_BLOCK if T % Q_BLOCK == 0 else T
    nb = T // qb
    scale = (NOPE_DIM + ROPE_DIM) ** -0.5
    k_chunk = k_pos // CHUNK

    def one_block(args):
        qn, qr, qp = args
        s = (jnp.einsum('bqhd,bkhd->bhqk', qn, k_nope, preferred_element_type=jnp.float32)
             + jnp.einsum('bqhd,bkd->bhqk', qr, k_rope, preferred_element_type=jnp.float32)) * scale
        mask = k_chunk[None, :] <= (qp // CHUNK)[:, None]
        p = jax.nn.softmax(jnp.where(mask, s, -jnp.inf), axis=-1).astype(v.dtype)
        return jnp.einsum('bhqk,bkhv->bqhv', p, v)

    def split_blocks(t):
        return t.reshape((B, nb, qb) + t.shape[2:]).swapaxes(0, 1)

    o = lax.map(one_block, (split_blocks(q_nope), split_blocks(q_rope), q_pos.reshape(nb, qb)))
    return o.swapaxes(0, 1).reshape(B, T, H, V_DIM)


def mla_mixer(h, q_pos, k_pos, w_dq, q_a_gain, w_uq, q_nope_gain, q_rope_gain, w_o, k_nope, k_rope, v):
    B, T, _ = h.shape
    q = (rms_norm(h @ w_dq, q_a_gain) @ w_uq).reshape(B, T, MLA_HEADS, NOPE_DIM + ROPE_DIM)
    q_nope, q_rope = jnp.split(q, [NOPE_DIM], axis=-1)
    q_nope = rms_norm(q_nope, q_nope_gain)
    q_rope = apply_rope(rms_norm(q_rope, q_rope_gain), q_pos)
    o = block_causal_attention(q_nope, q_rope, k_nope, k_rope, v, q_pos, k_pos)
    return o.reshape(B, T, MLA_HEADS * V_DIM) @ w_o


def moe_ffn(h, w_router, router_bias, w_gate_up, w_down, w_sh_gate_up, w_sh_down):
    B, T, D = h.shape
    ht = h.reshape(B * T, D)
    scores = jax.nn.sigmoid((ht @ w_router).astype(jnp.float32))
    biased = scores + router_bias.astype(jnp.float32)
    group_score = lax.top_k(biased.reshape(-1, N_GROUPS, EXPERTS_PER_GROUP), TOP_K)[0].sum(-1)
    best_group = jnp.argmax(group_score, axis=-1)
    in_group = (jnp.arange(N_EXPERTS) // EXPERTS_PER_GROUP)[None, :] == best_group[:, None]
    _, idx = lax.top_k(jnp.where(in_group, biased, -jnp.inf), TOP_K)
    sel = jnp.take_along_axis(scores, idx, axis=-1)
    wts = sel / jnp.sum(sel, axis=-1, keepdims=True)
    gate = jnp.einsum('nk,nke->ne', wts, jax.nn.one_hot(idx, N_EXPERTS, dtype=jnp.float32))
    gu = jnp.einsum('nd,edf->nef', ht, w_gate_up)
    g_, u_ = jnp.split(gu, 2, axis=-1)
    act = jax.nn.silu(g_) * u_ * gate[:, :, None].astype(ht.dtype)
    routed = jnp.einsum('nef,efd->nd', act, w_down)
    sg, su = jnp.split(ht @ w_sh_gate_up, 2, axis=-1)
    shared = (jax.nn.silu(sg) * su) @ w_sh_down
    return (routed + shared).reshape(B, T, D)


def trunk(x, c, gla_state, cache_ckv, cache_krope, weights):
    (w_ada, b_ada, norm_mix_gain, norm_ffn_gain,
     gla_w_in, gla_w_alpha2, gla_b_alpha, gla_out_gain, gla_w_o,
     kv_w_ada, kv_b_ada, kv_norm_gain, mla_w_dkv, mla_ckv_gain, mla_k_rope_gain, mla_w_ukv, mla_k_nope_gain,
     mla_w_dq, mla_q_a_gain, mla_w_uq, mla_q_nope_gain, mla_q_rope_gain, mla_w_o,
     moe_w_router, moe_router_bias, moe_w_gate_up, moe_w_down, moe_w_shared_gate_up, moe_w_shared_down) = weights
    T = x.shape[1]
    offset = 0 if cache_ckv is None else cache_ckv.shape[1]
    q_pos = offset + jnp.arange(T, dtype=jnp.int32)
    c_act = jax.nn.silu(c)
    new_gla = []
    k_nope = v = k_rope_all = k_pos = ckv_new = krope_new = None
    for layer in range(DEPTH):
        mod = c_act @ w_ada[layer] + b_ada[layer]
        sh_m, sc_m, g_m, sh_f, sc_f, g_f = jnp.split(mod, 6, axis=-1)
        h = ada_rmsnorm(x, norm_mix_gain[layer], sh_m, sc_m)
        if layer < N_A_LAYERS:
            y, s_new = gla_mixer(h, gla_w_in[layer], gla_w_alpha2[layer], gla_b_alpha[layer],
                                 gla_out_gain[layer], gla_w_o[layer], gla_state[layer])
            new_gla.append(s_new)
        else:
            j = layer - N_A_LAYERS
            y = mla_mixer(h, q_pos, k_pos, mla_w_dq[j], mla_q_a_gain[j], mla_w_uq[j],
                          mla_q_nope_gain[j], mla_q_rope_gain[j], mla_w_o[j], k_nope, k_rope_all, v)
        x = x + g_m[:, None, :] * y
        h = ada_rmsnorm(x, norm_ffn_gain[layer], sh_f, sc_f)
        x = x + g_f[:, None, :] * moe_ffn(h, moe_w_router, moe_router_bias, moe_w_gate_up[layer],
                                          moe_w_down[layer], moe_w_shared_gate_up[layer],
                                          moe_w_shared_down[layer])
        if layer == N_A_LAYERS - 1:
            sh_kv, sc_kv = jnp.split(c_act @ kv_w_ada + kv_b_ada, 2, axis=-1)
            h_kv = ada_rmsnorm(x, kv_norm_gain, sh_kv, sc_kv)
            ckv_raw, kr_raw = jnp.split(h_kv @ mla_w_dkv, [KV_LORA], axis=-1)
            ckv_new = rms_norm(ckv_raw, mla_ckv_gain)
            krope_new = apply_rope(rms_norm(kr_raw, mla_k_rope_gain), q_pos)
            if cache_ckv is None:
                ckv_all, k_rope_all = ckv_new, krope_new
            else:
                ckv_all = jnp.concatenate([cache_ckv, ckv_new], axis=1)
                k_rope_all = jnp.concatenate([cache_krope, krope_new], axis=1)
            k_nope, v = mla_expand_kv(ckv_all, mla_w_ukv, mla_k_nope_gain)
            k_pos = jnp.arange(ckv_all.shape[1], dtype=jnp.int32)
    return x, jnp.stack(new_gla), ckv_new, krope_new


def setup_inputs(seed: int = 0) -> dict:
    key = jax.random.key(seed)
    ks = iter(jax.random.split(key, 64))
    f32 = jnp.float32

    def nrm(shape, fan_in, scale=1.0):
        return jax.random.normal(next(ks), shape, f32) * (scale * fan_in ** -0.5)

    def gain(shape):
        return 1.0 + 0.05 * jax.random.normal(next(ks), shape, f32)

    def small(shape):
        return 0.02 * jax.random.normal(next(ks), shape, f32)

    D = D_MODEL
    return {
        'x_prompt': jax.random.normal(next(ks), (BATCH, SEQ, D), f32),
        'x_sample': jax.random.normal(next(ks), (DEC_BATCH, DEC_SEQ, D), f32),
        'c_prompt': jax.random.normal(next(ks), (BATCH, D), f32),
        'c_sample': jax.random.normal(next(ks), (DEC_BATCH, D), f32),
        'state_gla': 0.1 * jax.random.normal(next(ks), (N_A_LAYERS, DEC_BATCH, GLA_HEADS, GLA_DK_H, GLA_DV_H), f32),
        'cache_kv_latent': jax.random.normal(next(ks), (DEC_BATCH, PAST_LEN, KV_LORA), f32),
        'cache_k_rope': jax.random.normal(next(ks), (DEC_BATCH, PAST_LEN, ROPE_DIM), f32),
        'w_ada': nrm((DEPTH, D, 6 * D), D, 0.5),
        'b_ada': small((DEPTH, 6 * D)),
        'norm_mix_gain': gain((DEPTH, D)),
        'norm_ffn_gain': gain((DEPTH, D)),
        'gla_w_in': nrm((N_A_LAYERS, D, 2 * GLA_DK + 2 * GLA_DV + GLA_GATE_RANK), D),
        'gla_w_alpha2': nrm((N_A_LAYERS, GLA_GATE_RANK, GLA_DK), GLA_GATE_RANK),
        'gla_b_alpha': small((N_A_LAYERS, GLA_DK)),
        'gla_out_gain': gain((N_A_LAYERS, GLA_DV_H)),
        'gla_w_o': nrm((N_A_LAYERS, GLA_DV, D), GLA_DV),
        'kv_w_ada': nrm((D, 2 * D), D, 0.5),
        'kv_b_ada': small((2 * D,)),
        'kv_norm_gain': gain((D,)),
        'mla_w_dkv': nrm((D, KV_LORA + ROPE_DIM), D),
        'mla_ckv_gain': gain((KV_LORA,)),
        'mla_k_rope_gain': gain((ROPE_DIM,)),
        'mla_w_ukv': nrm((KV_LORA, MLA_HEADS * (NOPE_DIM + V_DIM)), KV_LORA),
        'mla_k_nope_gain': gain((NOPE_DIM,)),
        'mla_w_dq': nrm((N_B_LAYERS, D, Q_LORA), D),
        'mla_q_a_gain': gain((N_B_LAYERS, Q_LORA)),
        'mla_w_uq': nrm((N_B_LAYERS, Q_LORA, MLA_HEADS * (NOPE_DIM + ROPE_DIM)), Q_LORA),
        'mla_q_nope_gain': gain((N_B_LAYERS, NOPE_DIM)),
        'mla_q_rope_gain': gain((N_B_LAYERS, ROPE_DIM)),
        'mla_w_o': nrm((N_B_LAYERS, MLA_HEADS * V_DIM, D), MLA_HEADS * V_DIM),
        'moe_w_router': nrm((D, N_EXPERTS), D),
        'moe_router_bias': 0.01 * jax.random.normal(next(ks), (N_EXPERTS,), f32),
        'moe_w_gate_up': nrm((DEPTH, N_EXPERTS, D, 2 * EXPERT_FF), D),
        'moe_w_down': nrm((DEPTH, N_EXPERTS, EXPERT_FF, D), EXPERT_FF),
        'moe_w_shared_gate_up': nrm((DEPTH, D, 2 * SHARED_FF), D),
        'moe_w_shared_down': nrm((DEPTH, SHARED_FF, D), SHARED_FF),
    }


def reference(x_prompt, x_sample, c_prompt, c_sample, state_gla, cache_kv_latent, cache_k_rope,
              w_ada, b_ada, norm_mix_gain, norm_ffn_gain,
              gla_w_in, gla_w_alpha2, gla_b_alpha, gla_out_gain, gla_w_o,
              kv_w_ada, kv_b_ada, kv_norm_gain, mla_w_dkv, mla_ckv_gain, mla_k_rope_gain, mla_w_ukv,
              mla_k_nope_gain, mla_w_dq, mla_q_a_gain, mla_w_uq, mla_q_nope_gain, mla_q_rope_gain, mla_w_o,
              moe_w_router, moe_router_bias, moe_w_gate_up, moe_w_down, moe_w_shared_gate_up,
              moe_w_shared_down):
    weights = (w_ada, b_ada, norm_mix_gain, norm_ffn_gain,
               gla_w_in, gla_w_alpha2, gla_b_alpha, gla_out_gain, gla_w_o,
               kv_w_ada, kv_b_ada, kv_norm_gain, mla_w_dkv, mla_ckv_gain, mla_k_rope_gain, mla_w_ukv,
               mla_k_nope_gain, mla_w_dq, mla_q_a_gain, mla_w_uq, mla_q_nope_gain, mla_q_rope_gain, mla_w_o,
               moe_w_router, moe_router_bias, moe_w_gate_up, moe_w_down, moe_w_shared_gate_up,
               moe_w_shared_down)
    gla_zero = jnp.zeros((N_A_LAYERS, x_prompt.shape[0], GLA_HEADS, GLA_DK_H, GLA_DV_H), x_prompt.dtype)
    y_prompt, state_gla_prompt, kv_latent_prompt, k_rope_prompt = trunk(
        x_prompt, c_prompt, gla_zero, None, None, weights)
    y_sample, state_gla_sample, kv_latent_sample, k_rope_sample = trunk(
        x_sample, c_sample, state_gla, cache_kv_latent, cache_k_rope, weights)
    return (y_prompt, y_sample, state_gla_prompt, kv_latent_prompt, k_rope_prompt,
            state_gla_sample, kv_latent_sample, k_rope_sample)
```

```python
import functools
import math

import numpy as np
import jax
import jax.numpy as jnp
from jax import lax
from jax.experimental import pallas as pl
from jax.experimental.pallas import tpu as pltpu

F32 = jnp.float32
BF16 = jnp.bfloat16

D_MODEL = 1024
CHUNK = 64
CHUNK_SHIFT = 6
GLA_HEADS = 4
GLA_DK = 512
GLA_DV = 1024
GLA_DK_H = 128
GLA_DV_H = 256
GLA_GATE_RANK = 16
GLA_GATE_TAU = 16.0
MLA_HEADS = 16
Q_LORA = 256
KV_LORA = 128
NOPE_DIM = 64
ROPE_DIM = 32
ROPE_HALF = 16
V_DIM = 64
ROPE_BASE = 10000.0
N_EXPERTS = 16
N_GROUPS = 4
EXPERTS_PER_GROUP = 4
EXPERT_FF = 256
SHARED_FF = 256
EPS = 1e-6

LANES = 128
HEAD_BLOCK = LANES
HEAD_PAIRS = MLA_HEADS // 2
VMEM_LIMIT = 56 * 1024 * 1024


def _cparams(sem):
    return pltpu.CompilerParams(dimension_semantics=sem, vmem_limit_bytes=VMEM_LIMIT)


def _dot(a, b):
    return jnp.dot(a, b, preferred_element_type=F32)


def _dot_nt(a, b):
    return lax.dot_general(a, b, (((1,), (1,)), ((), ())), preferred_element_type=F32)


def _dot_tn(a, b):
    return lax.dot_general(a, b, (((0,), (0,)), ((), ())), preferred_element_type=F32)


def _silu(x):
    return x * jax.nn.sigmoid(x)


def _split2(x):
    hi = x.astype(BF16)
    lo = (x - hi.astype(F32)).astype(BF16)
    return hi, lo


def _ada_norm(x, gain, shift, scale):
    y = x * lax.rsqrt(jnp.mean(x * x, axis=-1, keepdims=True) + EPS)
    return (y * gain) * (1.0 + scale) + shift


def _const_spec(shape):
    nd = len(shape)
    return pl.BlockSpec(shape, lambda *_: (0,) * nd)


def _mod_spec(slot, n_slots, tiles_per_batch):
    return pl.BlockSpec((1, 1, D_MODEL),
                        lambda i, *_: ((i // tiles_per_batch) * n_slots + slot, 0, 0))


def _mod_kernel(c_ref, w_ref, b_ref, o_ref):
    c = c_ref[...]
    o_ref[0] = _dot(_silu(c).astype(BF16), w_ref[0].astype(BF16)) + b_ref[0]


def _modulation(c_pad, w, b):
    n_layers, _, n_out = w.shape
    rows = c_pad.shape[0]
    tn = 1024
    return pl.pallas_call(
        _mod_kernel,
        grid=(n_layers, n_out // tn),
        in_specs=[pl.BlockSpec((rows, D_MODEL), lambda l, j: (0, 0)),
                  pl.BlockSpec((1, D_MODEL, tn), lambda l, j: (l, 0, j)),
                  pl.BlockSpec((1, 1, tn), lambda l, j: (l, 0, j))],
        out_specs=pl.BlockSpec((1, rows, tn), lambda l, j: (l, 0, j)),
        out_shape=jax.ShapeDtypeStruct((n_layers, rows, n_out), F32),
        compiler_params=_cparams(("arbitrary", "arbitrary")),
        name="modulation",
    )(c_pad, w, b)


def _log_sigmoid(z):
    return jnp.minimum(z, 0.0) - jnp.log1p(jnp.exp(-jnp.abs(z)))


def _gla_pre_kernel(x_ref, sh_ref, sc_ref, gain_ref, w_ref, wa_ref, wa2_ref, ba_ref,
                    q_ref, k_ref, v_ref, r_ref, la_ref):
    h = _ada_norm(x_ref[...], gain_ref[...], sh_ref[0], sc_ref[0]).astype(BF16)
    y = _dot(h, w_ref[...])
    q_ref[...] = y[:, :GLA_DK] * (GLA_DK_H ** -0.5)
    k_ref[...] = y[:, GLA_DK:2 * GLA_DK]
    v_ref[...] = y[:, 2 * GLA_DK:2 * GLA_DK + GLA_DV].astype(BF16)
    r_ref[...] = y[:, 2 * GLA_DK + GLA_DV:]
    a = _dot(h, wa_ref[...])
    z = _dot(a.astype(BF16), wa2_ref[...]) + ba_ref[...]
    la_ref[...] = _log_sigmoid(z) * (1.0 / GLA_GATE_TAU)


def _gla_pre(x2, mod, n_slots, slot0, tm, tpb, gain, w_main, w_a, w_a2, b_a):
    n = x2.shape[0]
    row = lambda width: pl.BlockSpec((tm, width), lambda i: (i, 0))
    return pl.pallas_call(
        _gla_pre_kernel,
        grid=(n // tm,),
        in_specs=[row(D_MODEL), _mod_spec(slot0, n_slots, tpb), _mod_spec(slot0 + 1, n_slots, tpb),
                  _const_spec((1, D_MODEL)), _const_spec(w_main.shape), _const_spec(w_a.shape),
                  _const_spec(w_a2.shape), _const_spec((1, GLA_DK))],
        out_specs=[row(GLA_DK), row(GLA_DK), row(GLA_DV), row(GLA_DV), row(GLA_DK)],
        out_shape=[jax.ShapeDtypeStruct((n, GLA_DK), F32), jax.ShapeDtypeStruct((n, GLA_DK), F32),
                   jax.ShapeDtypeStruct((n, GLA_DV), BF16), jax.ShapeDtypeStruct((n, GLA_DV), F32),
                   jax.ShapeDtypeStruct((n, GLA_DK), F32)],
        compiler_params=_cparams(("arbitrary",)),
        name="gla_pre",
    )(x2, mod, mod, gain, w_main, w_a, w_a2, b_a)


def _gla_core_kernel(q_ref, k_ref, la_ref, v_ref, r_ref, gain_ref, s0_ref,
                     o_ref, st_ref, s_scr, *, chunk, n_chunks):
    j = pl.program_id(1)

    @pl.when(j == 0)
    def _():
        s_scr[...] = s0_ref[0]

    ri = lax.broadcasted_iota(jnp.int32, (chunk, chunk), 0)
    ci = lax.broadcasted_iota(jnp.int32, (chunk, chunk), 1)
    causal = ri >= ci
    tri = causal.astype(BF16)
    gain = gain_ref[...]

    def body(c, carry):
        rows = pl.ds(pl.multiple_of(c * chunk, chunk), chunk)
        g = la_ref[rows, :]
        g_hi = g.astype(BF16)
        g_r1 = g - g_hi.astype(F32)
        g_mid = g_r1.astype(BF16)
        g_lo = (g_r1 - g_mid.astype(F32)).astype(BF16)
        b = _dot(tri, g_hi) + _dot(tri, g_mid) + _dot(tri, g_lo)
        b_last = b[chunk - 1:chunk, :]
        q = q_ref[rows, :]
        k = k_ref[rows, :]
        q_dec = (q * jnp.exp(b)).astype(BF16)
        k_inv = (k * jnp.exp(-b)).astype(BF16)
        k_end = (k * jnp.exp(b_last - b)).astype(BF16)
        dec = jnp.exp(b_last)
        for h in range(GLA_HEADS):
            ks = slice(h * GLA_DK_H, (h + 1) * GLA_DK_H)
            vs = slice(h * GLA_DV_H, (h + 1) * GLA_DV_H)
            v_h = v_ref[rows, vs]
            s_t = s_scr[h]
            sc = jnp.where(causal, _dot_nt(q_dec[:, ks], k_inv[:, ks]), 0.0)
            o = _dot(sc.astype(BF16), v_h) + _dot_nt(q_dec[:, ks], s_t.astype(BF16))
            s_scr[h] = s_t * dec[:, ks] + _dot_tn(v_h, k_end[:, ks])
            on = o * lax.rsqrt(jnp.mean(o * o, axis=-1, keepdims=True) + EPS) * gain
            out = (on * _silu(r_ref[rows, vs])).astype(BF16)
            o_ref[0, 2 * h, rows, :] = out[:, :LANES]
            o_ref[0, 2 * h + 1, rows, :] = out[:, LANES:]
        return carry

    lax.fori_loop(0, n_chunks, body, 0)

    @pl.when(j == pl.num_programs(1) - 1)
    def _():
        st_ref[0] = s_scr[...]


def _gla_core(q, k, la, v, r, gain, s0_t, batch, seq, chunk, rows_per_step):
    n_chunks = rows_per_step // chunk
    steps = seq // rows_per_step
    row = lambda width: pl.BlockSpec((rows_per_step, width), lambda b, j: (b * steps + j, 0))
    state = pl.BlockSpec((1, GLA_HEADS, GLA_DV_H, GLA_DK_H), lambda b, j: (b, 0, 0, 0))
    return pl.pallas_call(
        functools.partial(_gla_core_kernel, chunk=chunk, n_chunks=n_chunks),
        grid=(batch, steps),
        in_specs=[row(GLA_DK), row(GLA_DK), row(GLA_DK), row(GLA_DV), row(GLA_DV),
                  _const_spec((1, GLA_DV_H)), state],
        out_specs=[pl.BlockSpec((1, HEAD_PAIRS, rows_per_step, LANES), lambda b, j: (b, 0, j, 0)),
                   state],
        out_shape=[jax.ShapeDtypeStruct((batch, HEAD_PAIRS, seq, LANES), BF16),
                   jax.ShapeDtypeStruct((batch, GLA_HEADS, GLA_DV_H, GLA_DK_H), F32)],
        scratch_shapes=[pltpu.VMEM((GLA_HEADS, GLA_DV_H, GLA_DK_H), F32)],
        compiler_params=_cparams(("arbitrary", "arbitrary")),
        name="gla_core",
    )(q, k, la, v, r, gain, s0_t)


def _top2_sum(a, b, c, d):
    hi1, lo1 = jnp.maximum(a, b), jnp.minimum(a, b)
    hi2, lo2 = jnp.maximum(c, d), jnp.minimum(c, d)
    return jnp.maximum(hi1, hi2) + jnp.maximum(jnp.minimum(hi1, hi2), jnp.maximum(lo1, lo2))


def _first_argmax(vals):
    idx = jnp.zeros(vals[0].shape, jnp.int32)
    best = vals[0]
    for j in range(1, len(vals)):
        upd = vals[j] > best
        idx = jnp.where(upd, j, idx)
        best = jnp.where(upd, vals[j], best)
    return idx


def _pick(idx, vals):
    out = vals[-1]
    for j in range(len(vals) - 2, -1, -1):
        out = jnp.where(idx == j, vals[j], out)
    return out


def _router_gates_t(logits_t, bias_col):
    s_t = jax.nn.sigmoid(logits_t)
    b_t = s_t + bias_col
    s = [s_t[e:e + 1, :] for e in range(N_EXPERTS)]
    b = [b_t[e:e + 1, :] for e in range(N_EXPERTS)]
    group_score = [_top2_sum(*b[4 * g:4 * g + 4]) for g in range(N_GROUPS)]
    best = _first_argmax(group_score)
    vb = [_pick(best, [b[4 * g + j] for g in range(N_GROUPS)]) for j in range(EXPERTS_PER_GROUP)]
    vs = [_pick(best, [s[4 * g + j] for g in range(N_GROUPS)]) for j in range(EXPERTS_PER_GROUP)]
    i1 = _first_argmax(vb)
    i2 = _first_argmax([jnp.where(i1 == j, -jnp.inf, vb[j]) for j in range(EXPERTS_PER_GROUP)])
    sel1 = _pick(i1, vs)
    sel2 = _pick(i2, vs)
    den = sel1 + sel2
    w1 = sel1 / den
    w2 = sel2 / den
    gates = []
    for g in range(N_GROUPS):
        for j in range(EXPERTS_PER_GROUP):
            val = jnp.where(i1 == j, w1, 0.0) + jnp.where(i2 == j, w2, 0.0)
            gates.append(jnp.where(best == g, val, 0.0))
    return gates


def _post_kernel(x_ref, y_ref, wo_ref, gm_ref, shf_ref, scf_ref, gf_ref, gain_ref,
                 wr_ref, rb_ref, wgu_ref, wd_ref, wsgu_ref, wsd_ref,
                 o_ref, xn_scr, h_scr, gate_scr, acc_scr):
    e = pl.program_id(1)
    tm = x_ref.shape[0]

    @pl.when(e == 0)
    def _():
        y = jnp.concatenate([y_ref[0, p] for p in range(HEAD_PAIRS)], axis=-1)
        xn = x_ref[...] + gm_ref[0] * _dot(y, wo_ref[...])
        xn_scr[...] = xn
        h = _ada_norm(xn, gain_ref[...], shf_ref[0], scf_ref[0]).astype(BF16)
        h_scr[...] = h
        gates = _router_gates_t(_dot_nt(wr_ref[...], h), rb_ref[...])
        gate_t = jnp.concatenate(gates + [jnp.zeros((LANES - N_EXPERTS, tm), F32)], axis=0)
        gate_scr[...] = gate_t.T
        sgu = _dot(h, wsgu_ref[...])
        act = _silu(sgu[:, :SHARED_FF]) * sgu[:, SHARED_FF:]
        acc_scr[...] = _dot(act.astype(BF16), wsd_ref[...])

    gu = _dot(h_scr[...], wgu_ref[0])
    lane = lax.broadcasted_iota(jnp.int32, (tm, LANES), 1)
    gcol = jnp.sum(jnp.where(lane == e, gate_scr[...], 0.0), axis=-1, keepdims=True)
    act = _silu(gu[:, :EXPERT_FF]) * gu[:, EXPERT_FF:] * gcol
    acc_scr[...] += _dot(act.astype(BF16), wd_ref[0])

    @pl.when(e == N_EXPERTS - 1)
    def _():
        o_ref[...] = xn_scr[...] + gf_ref[0] * acc_scr[...]


def _post(x2, y, mod, n_slots, slot0, tm, tpb, w_o, gain_f, wr_t, rbias, wgu, wd, wsgu, wsd):
    n = x2.shape[0]
    ms = lambda s: pl.BlockSpec((1, 1, D_MODEL),
                                lambda i, e: ((i // tpb) * n_slots + slot0 + s, 0, 0))
    cs = lambda shape: pl.BlockSpec(shape, lambda i, e: (0,) * len(shape))
    return pl.pallas_call(
        _post_kernel,
        grid=(n // tm, N_EXPERTS),
        in_specs=[pl.BlockSpec((tm, D_MODEL), lambda i, e: (i, 0)),
                  pl.BlockSpec((1, HEAD_PAIRS, tm, LANES), lambda i, e: (i // tpb, 0, i % tpb, 0)),
                  cs((D_MODEL, D_MODEL)),
                  ms(2), ms(3), ms(4), ms(5), cs((1, D_MODEL)),
                  cs((N_EXPERTS, D_MODEL)), cs((N_EXPERTS, 1)),
                  pl.BlockSpec((1, D_MODEL, 2 * EXPERT_FF), lambda i, e: (e, 0, 0)),
                  pl.BlockSpec((1, EXPERT_FF, D_MODEL), lambda i, e: (e, 0, 0)),
                  cs((D_MODEL, 2 * SHARED_FF)), cs((SHARED_FF, D_MODEL))],
        out_specs=pl.BlockSpec((tm, D_MODEL), lambda i, e: (i, 0)),
        out_shape=jax.ShapeDtypeStruct((n, D_MODEL), F32),
        scratch_shapes=[pltpu.VMEM((tm, D_MODEL), F32), pltpu.VMEM((tm, D_MODEL), BF16),
                        pltpu.VMEM((tm, LANES), F32), pltpu.VMEM((tm, D_MODEL), F32)],
        compiler_params=_cparams(("arbitrary", "arbitrary")),
        name="post_moe",
    )(x2, y, w_o, mod, mod, mod, mod, gain_f, wr_t, rbias, wgu, wd, wsgu, wsd)


def _rope_block(x, cos, sin_a, sin_b):
    return (x * cos + pltpu.roll(x, LANES - ROPE_HALF, 1) * sin_a
            + pltpu.roll(x, ROPE_HALF, 1) * sin_b)


def _kv_latent_kernel(x_ref, sh_ref, sc_ref, gain_ref, wc_ref, wr_ref, cg_ref, rg_ref,
                      cos_ref, sa_ref, sb_ref, ckv_ref, kr_ref):
    h = _ada_norm(x_ref[...], gain_ref[...], sh_ref[0], sc_ref[0]).astype(BF16)
    c_raw = _dot(h, wc_ref[...])
    ckv_ref[...] = c_raw * lax.rsqrt(jnp.mean(c_raw * c_raw, axis=-1, keepdims=True) + EPS) * cg_ref[...]
    r_raw = _dot(h, wr_ref[...])
    ms = jnp.sum(r_raw * r_raw, axis=-1, keepdims=True) * (1.0 / ROPE_DIM)
    rn = r_raw * lax.rsqrt(ms + EPS) * rg_ref[...]
    kr_ref[...] = _rope_block(rn, cos_ref[...], sa_ref[...], sb_ref[...])


def _kv_latent(x2, mod, tm, tpb, gain, w_c, w_r, c_gain, r_gain, cos, sin_a, sin_b):
    n = x2.shape[0]
    row = lambda width: pl.BlockSpec((tm, width), lambda i: (i, 0))
    tab = pl.BlockSpec((tm, LANES), lambda i: (i % tpb, 0))
    return pl.pallas_call(
        _kv_latent_kernel,
        grid=(n // tm,),
        in_specs=[row(D_MODEL), _mod_spec(0, 2, tpb), _mod_spec(1, 2, tpb), _const_spec((1, D_MODEL)),
                  _const_spec((D_MODEL, KV_LORA)), _const_spec((D_MODEL, LANES)),
                  _const_spec((1, KV_LORA)), _const_spec((1, LANES)), tab, tab, tab],
        out_specs=[row(KV_LORA), row(LANES)],
        out_shape=[jax.ShapeDtypeStruct((n, KV_LORA), F32), jax.ShapeDtypeStruct((n, LANES), F32)],
        compiler_params=_cparams(("arbitrary",)),
        name="kv_latent",
    )(x2, mod, mod, gain, w_c, w_r, c_gain, r_gain, cos, sin_a, sin_b)


def _segment_inv_rms(x, seg_ref, exp_ref, inv_count):
    hi, lo = _split2(x * x)
    ss = _dot(hi, seg_ref[...]) + _dot(lo, seg_ref[...])
    inv = lax.rsqrt(ss * inv_count + EPS)
    ihi, ilo = _split2(inv)
    return _dot(ihi, exp_ref[...]) + _dot(ilo, exp_ref[...])


def _kv_expand_kernel(c_ref, kr_ref, wk_ref, wvt_ref, seg_ref, exp_ref, cnt_ref, gl_ref, place_ref,
                      k_ref, vt_ref):
    c = c_ref[0].astype(BF16)
    kn = _dot(c, wk_ref[...])
    inv = _segment_inv_rms(kn, seg_ref, exp_ref, cnt_ref[...])
    k = kn * inv * gl_ref[...] + _dot(kr_ref[0].astype(BF16), place_ref[...])
    kb = k.astype(BF16)
    vt = _dot_nt(wvt_ref[...], c).astype(BF16)
    for h in range(MLA_HEADS):
        k_ref[0, h] = kb[:, h * HEAD_BLOCK:(h + 1) * HEAD_BLOCK]
        vt_ref[0, h] = vt[h * V_DIM:(h + 1) * V_DIM, :]


def _kv_expand(ckv, kr, tm, w_k, w_vt, seg, exp, cnt, gain_lane, place):
    batch, lp, _ = ckv.shape
    width = MLA_HEADS * HEAD_BLOCK
    cs = lambda shape: pl.BlockSpec(shape, lambda b, i: (0,) * len(shape))
    return pl.pallas_call(
        _kv_expand_kernel,
        grid=(batch, lp // tm),
        in_specs=[pl.BlockSpec((1, tm, KV_LORA), lambda b, i: (b, i, 0)),
                  pl.BlockSpec((1, tm, LANES), lambda b, i: (b, i, 0)),
                  cs((KV_LORA, width)), cs((MLA_HEADS * V_DIM, KV_LORA)),
                  cs((width, LANES)), cs((LANES, width)), cs((1, LANES)), cs((1, width)),
                  cs((LANES, width))],
        out_specs=[pl.BlockSpec((1, MLA_HEADS, tm, HEAD_BLOCK), lambda b, i: (b, 0, i, 0)),
                   pl.BlockSpec((1, MLA_HEADS, V_DIM, tm), lambda b, i: (b, 0, 0, i))],
        out_shape=[jax.ShapeDtypeStruct((batch, MLA_HEADS, lp, HEAD_BLOCK), BF16),
                   jax.ShapeDtypeStruct((batch, MLA_HEADS, V_DIM, lp), BF16)],
        compiler_params=_cparams(("arbitrary", "arbitrary")),
        name="kv_expand",
    )(ckv, kr, w_k, w_vt, seg, exp, cnt, gain_lane, place)


def _mla_pre_kernel(x_ref, sh_ref, sc_ref, gain_ref, wdq_ref, qag_ref, wuq_ref, seg_ref, exp_ref,
                    cnt_ref, gl_ref, cos_ref, sa_ref, sb_ref, q_ref):
    h = _ada_norm(x_ref[...], gain_ref[...], sh_ref[0], sc_ref[0]).astype(BF16)
    qa = _dot(h, wdq_ref[...])
    qa = qa * lax.rsqrt(jnp.mean(qa * qa, axis=-1, keepdims=True) + EPS) * qag_ref[...]
    q = _dot(qa.astype(BF16), wuq_ref[...])
    qn = q * _segment_inv_rms(q, seg_ref, exp_ref, cnt_ref[...]) * gl_ref[...]
    cos, sa, sb = cos_ref[...], sa_ref[...], sb_ref[...]
    for hd in range(MLA_HEADS):
        blk = qn[:, hd * HEAD_BLOCK:(hd + 1) * HEAD_BLOCK]
        q_ref[0, hd] = _rope_block(blk, cos, sa, sb).astype(BF16)


def _mla_pre(x2, mod, n_slots, slot0, tm, tpb, batch, seq, gain, w_dq, qa_gain, w_uq, seg, exp, cnt,
             gain_lane, cos, sin_a, sin_b):
    n = x2.shape[0]
    width = MLA_HEADS * HEAD_BLOCK
    tab = pl.BlockSpec((tm, LANES), lambda i: (i % tpb, 0))
    return pl.pallas_call(
        _mla_pre_kernel,
        grid=(n // tm,),
        in_specs=[pl.BlockSpec((tm, D_MODEL), lambda i: (i, 0)),
                  _mod_spec(slot0, n_slots, tpb), _mod_spec(slot0 + 1, n_slots, tpb),
                  _const_spec((1, D_MODEL)), _const_spec((D_MODEL, Q_LORA)), _const_spec((1, Q_LORA)),
                  _const_spec((Q_LORA, width)), _const_spec((width, LANES)), _const_spec((LANES, width)),
                  _const_spec((1, LANES)), _const_spec((1, width)), tab, tab, tab],
        out_specs=pl.BlockSpec((1, MLA_HEADS, tm, HEAD_BLOCK), lambda i: (i // tpb, 0, i % tpb, 0)),
        out_shape=jax.ShapeDtypeStruct((batch, MLA_HEADS, seq, HEAD_BLOCK), BF16),
        compiler_params=_cparams(("arbitrary",)),
        name="mla_pre",
    )(x2, mod, mod, gain, w_dq, qa_gain, w_uq, seg, exp, cnt, gain_lane, cos, sin_a, sin_b)


def _flash_kernel(qi_ref, ki_ref, last_ref, q_ref, k_ref, vt_ref, o_ref, m_scr, l_scr, acc_scr,
                  *, tq, tk, q_offset, n_keys):
    p = pl.program_id(1)
    qi = qi_ref[p]
    ki = ki_ref[p]
    q_lo = q_offset + qi * tq
    k_lo = ki * tk

    @pl.when(ki == 0)
    def _():
        m_scr[...] = jnp.full(m_scr.shape, -jnp.inf, F32)
        l_scr[...] = jnp.zeros(l_scr.shape, F32)
        acc_scr[...] = jnp.zeros(acc_scr.shape, F32)

    def step(masked):
        if masked:
            kidx = k_lo + lax.broadcasted_iota(jnp.int32, (tk, tq), 0)
            qpos = q_lo + lax.broadcasted_iota(jnp.int32, (tk, tq), 1)
            ok = (jnp.right_shift(kidx, CHUNK_SHIFT) <= jnp.right_shift(qpos, CHUNK_SHIFT)) & (kidx < n_keys)

        def pair(hp, carry):
            for hh in range(2):
                hd = 2 * hp + hh
                s_t = _dot_nt(k_ref[0, hd], q_ref[0, hd])
                if masked:
                    s_t = jnp.where(ok, s_t, -jnp.inf)
                m_old = m_scr[hd]
                m_new = jnp.maximum(m_old, jnp.max(s_t, axis=0, keepdims=True))
                alpha = jnp.exp(m_old - m_new)
                pr = jnp.exp(s_t - m_new)
                l_scr[hd] = alpha * l_scr[hd] + jnp.sum(pr, axis=0, keepdims=True)
                m_scr[hd] = m_new
                pv = _dot(vt_ref[0, hd], pr.astype(BF16))
                rows = slice(hh * V_DIM, (hh + 1) * V_DIM)
                acc_scr[hp, rows, :] = acc_scr[hp, rows, :] * alpha + pv
            return carry

        lax.fori_loop(0, HEAD_PAIRS, pair, 0)

    full = ((jnp.right_shift(k_lo + tk - 1, CHUNK_SHIFT) <= jnp.right_shift(q_lo, CHUNK_SHIFT))
            & (k_lo + tk <= n_keys))

    @pl.when(full)
    def _():
        step(False)

    @pl.when(jnp.logical_not(full))
    def _():
        step(True)

    @pl.when(last_ref[p] == 1)
    def _():
        def fin(hp, carry):
            inv = jnp.concatenate(
                [jnp.broadcast_to(1.0 / l_scr[2 * hp], (V_DIM, tq)),
                 jnp.broadcast_to(1.0 / l_scr[2 * hp + 1], (V_DIM, tq))], axis=0)
            o_ref[0, hp] = (acc_scr[hp] * inv).T.astype(BF16)
            return carry
        lax.fori_loop(0, HEAD_PAIRS, fin, 0)


def _flash_schedule(n_q, tq, tk, q_offset, n_keys, n_kt):
    qi_l, ki_l, last_l = [], [], []
    for qi in range(n_q):
        q_hi_chunk = (q_offset + qi * tq + tq - 1) // CHUNK
        k_max = min(n_keys, (q_hi_chunk + 1) * CHUNK)
        kt = min(n_kt, -(-k_max // tk))
        for ki in range(kt):
            qi_l.append(qi)
            ki_l.append(ki)
            last_l.append(1 if ki == kt - 1 else 0)
    return (jnp.asarray(np.array(qi_l, np.int32)), jnp.asarray(np.array(ki_l, np.int32)),
            jnp.asarray(np.array(last_l, np.int32)))


def _flash(q, k, vt, tq, tk, q_offset, n_keys):
    batch, _, seq, _ = q.shape
    lp = k.shape[2]
    qi_a, ki_a, last_a = _flash_schedule(seq // tq, tq, tk, q_offset, n_keys, lp // tk)
    n_steps = int(qi_a.shape[0])
    grid_spec = pltpu.PrefetchScalarGridSpec(
        num_scalar_prefetch=3,
        grid=(batch, n_steps),
        in_specs=[pl.BlockSpec((1, MLA_HEADS, tq, HEAD_BLOCK), lambda b, p, qi, ki, la: (b, 0, qi[p], 0)),
                  pl.BlockSpec((1, MLA_HEADS, tk, HEAD_BLOCK), lambda b, p, qi, ki, la: (b, 0, ki[p], 0)),
                  pl.BlockSpec((1, MLA_HEADS, V_DIM, tk), lambda b, p, qi, ki, la: (b, 0, 0, ki[p]))],
        out_specs=pl.BlockSpec((1, HEAD_PAIRS, tq, LANES), lambda b, p, qi, ki, la: (b, 0, qi[p], 0)),
        scratch_shapes=[pltpu.VMEM((MLA_HEADS, 1, tq), F32), pltpu.VMEM((MLA_HEADS, 1, tq), F32),
                        pltpu.VMEM((HEAD_PAIRS, 2 * V_DIM, tq), F32)],
    )
    return pl.pallas_call(
        functools.partial(_flash_kernel, tq=tq, tk=tk, q_offset=q_offset, n_keys=n_keys),
        grid_spec=grid_spec,
        out_shape=jax.ShapeDtypeStruct((batch, HEAD_PAIRS, seq, LANES), BF16),
        compiler_params=_cparams(("arbitrary", "arbitrary")),
        name="flash",
    )(qi_a, ki_a, last_a, q, k, vt)


def _head_block_tables():
    width = MLA_HEADS * HEAD_BLOCK
    seg_q = np.zeros((width, LANES), np.float32)
    seg_k = np.zeros((width, LANES), np.float32)
    place = np.zeros((LANES, width), np.float32)
    for h in range(MLA_HEADS):
        base = h * HEAD_BLOCK
        seg_q[base:base + NOPE_DIM, h] = 1.0
        seg_q[base + NOPE_DIM:base + NOPE_DIM + ROPE_DIM, MLA_HEADS + h] = 1.0
        seg_k[base:base + NOPE_DIM, h] = 1.0
        for d in range(ROPE_DIM):
            place[d, base + NOPE_DIM + d] = 1.0
    cnt_q = np.ones((1, LANES), np.float32)
    cnt_q[0, :MLA_HEADS] = 1.0 / NOPE_DIM
    cnt_q[0, MLA_HEADS:2 * MLA_HEADS] = 1.0 / ROPE_DIM
    cnt_k = np.ones((1, LANES), np.float32)
    cnt_k[0, :MLA_HEADS] = 1.0 / NOPE_DIM
    return seg_q, seg_k, place, cnt_q, cnt_k


def _to_head_blocks(w, per_head, parts):
    k_dim = w.shape[0]
    w3 = w.reshape(k_dim, MLA_HEADS, per_head)
    out = jnp.zeros((k_dim, MLA_HEADS, HEAD_BLOCK), w.dtype)
    for start, size, dest in parts:
        out = out.at[:, :, dest:dest + size].set(w3[:, :, start:start + size])
    return out.reshape(k_dim, MLA_HEADS * HEAD_BLOCK)


def _lane_gain(nope_gain, rope_gain, scale):
    blk = jnp.zeros((HEAD_BLOCK,), F32).at[:NOPE_DIM].set(nope_gain * scale)
    if rope_gain is not None:
        blk = blk.at[NOPE_DIM:NOPE_DIM + ROPE_DIM].set(rope_gain * scale)
    return jnp.tile(blk, MLA_HEADS).reshape(1, MLA_HEADS * HEAD_BLOCK)


def _rope_tables(positions, lane0):
    inv_freq = ROPE_BASE ** (-jnp.arange(ROPE_HALF, dtype=F32) / ROPE_HALF)
    ang = positions.astype(F32)[:, None] * inv_freq[None, :]
    cos, sin = jnp.cos(ang), jnp.sin(ang)
    t = positions.shape[0]
    z = jnp.zeros((t, LANES), F32)
    cos_t = z.at[:, lane0:lane0 + ROPE_HALF].set(cos).at[:, lane0 + ROPE_HALF:lane0 + ROPE_DIM].set(cos)
    if lane0 > 0:
        cos_t = cos_t.at[:, :lane0].set(1.0)
    sin_a = z.at[:, lane0:lane0 + ROPE_HALF].set(-sin)
    sin_b = z.at[:, lane0 + ROPE_HALF:lane0 + ROPE_DIM].set(sin)
    return cos_t, sin_a, sin_b


def _prep_weights(p):
    w = {}
    n_qkvr = 2 * GLA_DK + 2 * GLA_DV
    w["gla_main"] = p["gla_w_in"][:, :, :n_qkvr].astype(BF16)
    w["gla_a"] = jnp.pad(p["gla_w_in"][:, :, n_qkvr:], ((0, 0), (0, 0), (0, LANES - GLA_GATE_RANK))).astype(BF16)
    w["gla_a2"] = jnp.pad(p["gla_w_alpha2"], ((0, 0), (0, LANES - GLA_GATE_RANK), (0, 0))).astype(BF16)
    w["gla_ba"] = p["gla_b_alpha"].reshape(-1, 1, GLA_DK)
    w["gla_gain"] = p["gla_out_gain"].reshape(-1, 1, GLA_DV_H)
    w["gla_wo"] = p["gla_w_o"].astype(BF16)
    w["mla_wo"] = p["mla_w_o"].astype(BF16)
    w["wr_t"] = p["moe_w_router"].T.astype(BF16)
    w["rbias"] = p["moe_router_bias"].reshape(N_EXPERTS, 1)
    w["wgu"] = p["moe_w_gate_up"].astype(BF16)
    w["wd"] = p["moe_w_down"].astype(BF16)
    w["wsgu"] = p["moe_w_shared_gate_up"].astype(BF16)
    w["wsd"] = p["moe_w_shared_down"].astype(BF16)
    w["kv_wc"] = p["mla_w_dkv"][:, :KV_LORA].astype(BF16)
    w["kv_wr"] = jnp.pad(p["mla_w_dkv"][:, KV_LORA:], ((0, 0), (0, LANES - ROPE_DIM))).astype(BF16)
    w["ckv_gain"] = p["mla_ckv_gain"].reshape(1, KV_LORA)
    w["kr_gain"] = jnp.pad(p["mla_k_rope_gain"], (0, LANES - ROPE_DIM)).reshape(1, LANES)
    per_kv = NOPE_DIM + V_DIM
    w["w_uk"] = _to_head_blocks(p["mla_w_ukv"], per_kv, [(0, NOPE_DIM, 0)]).astype(BF16)
    w_uv = p["mla_w_ukv"].reshape(KV_LORA, MLA_HEADS, per_kv)[:, :, NOPE_DIM:]
    w["w_uv_t"] = w_uv.reshape(KV_LORA, MLA_HEADS * V_DIM).T.astype(BF16)
    per_q = NOPE_DIM + ROPE_DIM
    w["w_dq"] = p["mla_w_dq"].astype(BF16)
    w["w_uq"] = jnp.stack([
        _to_head_blocks(p["mla_w_uq"][j], per_q, [(0, per_q, 0)]) for j in range(p["mla_w_uq"].shape[0])
    ]).astype(BF16)
    scale = (NOPE_DIM + ROPE_DIM) ** -0.5
    w["q_lane_gain"] = jnp.stack([
        _lane_gain(p["mla_q_nope_gain"][j], p["mla_q_rope_gain"][j], scale)
        for j in range(p["mla_q_nope_gain"].shape[0])])
    w["k_lane_gain"] = _lane_gain(p["mla_k_nope_gain"], None, 1.0)
    seg_q, seg_k, place, cnt_q, cnt_k = _head_block_tables()
    w["seg_q"] = jnp.asarray(seg_q, BF16)
    w["exp_q"] = jnp.asarray(seg_q.T, BF16)
    w["seg_k"] = jnp.asarray(seg_k, BF16)
    w["exp_k"] = jnp.asarray(seg_k.T, BF16)
    w["place"] = jnp.asarray(place, BF16)
    w["cnt_q"] = jnp.asarray(cnt_q)
    w["cnt_k"] = jnp.asarray(cnt_k)
    return w


def _round_up(a, b):
    return -(-a // b) * b


def _trunk(x, mod_layers, mod_kv, gla_state, cache_ckv, cache_krope, p, w):
    batch, seq, _ = x.shape
    depth = mod_layers.shape[0]
    n_a = p["gla_w_in"].shape[0]
    n = batch * seq
    tm = min(512, seq)
    tpb = seq // tm
    x2 = x.reshape(n, D_MODEL)
    offset = 0 if cache_ckv is None else cache_ckv.shape[1]
    q_pos = offset + jnp.arange(seq, dtype=jnp.int32)
    kv_tabs = _rope_tables(q_pos, 0)
    q_tabs = _rope_tables(q_pos, NOPE_DIM)
    new_states = []
    ckv_new = krope_new = k_all = vt_all = None
    n_keys = 0
    tq = min(512, _round_up(seq, LANES))
    seq_q = _round_up(seq, tq)
    tk = 512
    for layer in range(depth):
        mod = mod_layers[layer].reshape(batch * 6, 1, D_MODEL)
        gain_m = p["norm_mix_gain"][layer].reshape(1, D_MODEL)
        gain_f = p["norm_ffn_gain"][layer].reshape(1, D_MODEL)
        if layer < n_a:
            q, k, v, r, la = _gla_pre(x2, mod, 6, 0, tm, tpb, gain_m, w["gla_main"][layer],
                                      w["gla_a"][layer], w["gla_a2"][layer], w["gla_ba"][layer])
            chunk = min(CHUNK, seq)
            rows_per_step = min(512, seq)
            s0_t = jnp.swapaxes(gla_state[layer], -1, -2)
            y, s_t = _gla_core(q, k, la, v, r, w["gla_gain"][layer], s0_t, batch, seq, chunk, rows_per_step)
            new_states.append(jnp.swapaxes(s_t, -1, -2))
            w_o = w["gla_wo"][layer]
        else:
            j = layer - n_a
            qh = _mla_pre(x2, mod, 6, 0, tm, tpb, batch, seq, gain_m, w["w_dq"][j],
                          p["mla_q_a_gain"][j].reshape(1, Q_LORA), w["w_uq"][j], w["seg_q"], w["exp_q"],
                          w["cnt_q"], w["q_lane_gain"][j], *q_tabs)
            if seq_q != seq:
                qh = jnp.pad(qh, ((0, 0), (0, 0), (0, seq_q - seq), (0, 0)))
            y = _flash(qh, k_all, vt_all, tq, tk, offset, n_keys)
            if seq_q != seq:
                y = y[:, :, :seq, :]
            w_o = w["mla_wo"][j]
        x2 = _post(x2, y, mod, 6, 0, tm, tpb, w_o, gain_f, w["wr_t"], w["rbias"], w["wgu"][layer],
                   w["wd"][layer], w["wsgu"][layer], w["wsd"][layer])
        if layer == n_a - 1:
            mkv = mod_kv.reshape(batch * 2, 1, D_MODEL)
            ckv2, kr2 = _kv_latent(x2, mkv, tm, tpb, p["kv_norm_gain"].reshape(1, D_MODEL), w["kv_wc"],
                                   w["kv_wr"], w["ckv_gain"], w["kr_gain"], *kv_tabs)
            ckv_new = ckv2.reshape(batch, seq, KV_LORA)
            kr_pad = kr2.reshape(batch, seq, LANES)
            krope_new = kr_pad[:, :, :ROPE_DIM]
            if cache_ckv is None:
                ckv_all, kr_all = ckv_new, kr_pad
            else:
                ckv_all = jnp.concatenate([cache_ckv, ckv_new], axis=1)
                kr_all = jnp.concatenate(
                    [jnp.pad(cache_krope, ((0, 0), (0, 0), (0, LANES - ROPE_DIM))), kr_pad], axis=1)
            n_keys = ckv_all.shape[1]
            lp = _round_up(n_keys, tk)
            if lp != n_keys:
                ckv_all = jnp.pad(ckv_all, ((0, 0), (0, lp - n_keys), (0, 0)))
                kr_all = jnp.pad(kr_all, ((0, 0), (0, lp - n_keys), (0, 0)))
            k_all, vt_all = _kv_expand(ckv_all, kr_all, tk, w["w_uk"], w["w_uv_t"], w["seg_k"], w["exp_k"],
                                       w["cnt_k"], w["k_lane_gain"], w["place"])
    return x2.reshape(batch, seq, D_MODEL), jnp.stack(new_states), ckv_new, krope_new


def kernel(x_prompt, x_sample, c_prompt, c_sample, state_gla, cache_kv_latent, cache_k_rope, w_ada, b_ada, norm_mix_gain, norm_ffn_gain, gla_w_in, gla_w_alpha2, gla_b_alpha, gla_out_gain, gla_w_o, kv_w_ada, kv_b_ada, kv_norm_gain, mla_w_dkv, mla_ckv_gain, mla_k_rope_gain, mla_w_ukv, mla_k_nope_gain, mla_w_dq, mla_q_a_gain, mla_w_uq, mla_q_nope_gain, mla_q_rope_gain, mla_w_o, moe_w_router, moe_router_bias, moe_w_gate_up, moe_w_down, moe_w_shared_gate_up, moe_w_shared_down):
    p = dict(norm_mix_gain=norm_mix_gain, norm_ffn_gain=norm_ffn_gain, gla_w_in=gla_w_in,
             gla_w_alpha2=gla_w_alpha2, gla_b_alpha=gla_b_alpha, gla_out_gain=gla_out_gain,
             gla_w_o=gla_w_o, kv_norm_gain=kv_norm_gain, mla_w_dkv=mla_w_dkv, mla_ckv_gain=mla_ckv_gain,
             mla_k_rope_gain=mla_k_rope_gain, mla_w_ukv=mla_w_ukv, mla_k_nope_gain=mla_k_nope_gain,
             mla_w_dq=mla_w_dq, mla_q_a_gain=mla_q_a_gain, mla_w_uq=mla_w_uq,
             mla_q_nope_gain=mla_q_nope_gain, mla_q_rope_gain=mla_q_rope_gain, mla_w_o=mla_w_o,
             moe_w_router=moe_w_router, moe_router_bias=moe_router_bias, moe_w_gate_up=moe_w_gate_up,
             moe_w_down=moe_w_down, moe_w_shared_gate_up=moe_w_shared_gate_up,
             moe_w_shared_down=moe_w_shared_down)
    w = _prep_weights(p)
    depth = w_ada.shape[0]
    bp, bs = c_prompt.shape[0], c_sample.shape[0]
    rows = _round_up(bp + bs, 8)
    c_all = jnp.concatenate([c_prompt, c_sample, jnp.zeros((rows - bp - bs, D_MODEL), F32)], axis=0)
    mod_l = _modulation(c_all, w_ada, b_ada.reshape(depth, 1, 6 * D_MODEL))
    mod_k = _modulation(c_all, kv_w_ada.reshape(1, D_MODEL, 2 * D_MODEL),
                        kv_b_ada.reshape(1, 1, 2 * D_MODEL))[0]
    gla_zero = jnp.zeros((state_gla.shape[0], bp) + state_gla.shape[2:], x_prompt.dtype)
    y_p, s_p, ckv_p, kr_p = _trunk(x_prompt, mod_l[:, :bp], mod_k[:bp], gla_zero, None, None, p, w)
    y_s, s_s, ckv_s, kr_s = _trunk(x_sample, mod_l[:, bp:bp + bs], mod_k[bp:bp + bs], state_gla,
                                   cache_kv_latent, cache_k_rope, p, w)
    return (y_p, y_s, s_p, ckv_p, kr_p, s_s, ckv_s, kr_s)
```

```python
import functools
import math

import numpy as np
import jax
import jax.numpy as jnp
from jax import lax
from jax.experimental import pallas as pl
from jax.experimental.pallas import tpu as pltpu

F32 = jnp.float32
BF16 = jnp.bfloat16

D_MODEL = 1024
CHUNK = 64
CHUNK_SHIFT = 6
GLA_HEADS = 4
GLA_DK = 512
GLA_DV = 1024
GLA_DK_H = 128
GLA_DV_H = 256
GLA_GATE_RANK = 16
GLA_GATE_TAU = 16.0
MLA_HEADS = 16
Q_LORA = 256
KV_LORA = 128
NOPE_DIM = 64
ROPE_DIM = 32
ROPE_HALF = 16
V_DIM = 64
ROPE_BASE = 10000.0
N_EXPERTS = 16
N_GROUPS = 4
EXPERTS_PER_GROUP = 4
EXPERT_FF = 256
SHARED_FF = 256
EPS = 1e-6

LANES = 128
HEAD_BLOCK = LANES
HEAD_PAIRS = MLA_HEADS // 2
BF16_SUBLANES = 16
V_AUG = V_DIM + BF16_SUBLANES
FLASH_HEAD_UNROLL = 16
EXPERTS_PER_STEP = 2
LOG2_E = math.log2(math.e)
LOGIT_LIMIT = 40.0
VMEM_LIMIT = 56 * 1024 * 1024


def _cparams(sem):
    return pltpu.CompilerParams(dimension_semantics=sem, vmem_limit_bytes=VMEM_LIMIT)


def _dot(a, b):
    return jnp.dot(a, b, preferred_element_type=F32)


def _dot_nt(a, b):
    return lax.dot_general(a, b, (((1,), (1,)), ((), ())), preferred_element_type=F32)


def _dot_tn(a, b):
    return lax.dot_general(a, b, (((0,), (0,)), ((), ())), preferred_element_type=F32)


def _silu(x):
    return x * jax.nn.sigmoid(x)


def _split2(x):
    hi = x.astype(BF16)
    lo = (x - hi.astype(F32)).astype(BF16)
    return hi, lo


def _ada_norm(x, gain, shift, scale):
    y = x * lax.rsqrt(jnp.mean(x * x, axis=-1, keepdims=True) + EPS)
    return (y * gain) * (1.0 + scale) + shift


def _const_spec(shape):
    nd = len(shape)
    return pl.BlockSpec(shape, lambda *_: (0,) * nd)


def _mod_spec(slot, n_slots, tiles_per_batch):
    return pl.BlockSpec((1, 1, D_MODEL),
                        lambda i, *_: ((i // tiles_per_batch) * n_slots + slot, 0, 0))


def _mod_kernel(c_ref, w_ref, b_ref, o_ref):
    c = c_ref[...]
    o_ref[0] = _dot(_silu(c).astype(BF16), w_ref[0].astype(BF16)) + b_ref[0]


def _modulation(c_pad, w, b):
    n_layers, _, n_out = w.shape
    rows = c_pad.shape[0]
    tn = 1024
    return pl.pallas_call(
        _mod_kernel,
        grid=(n_layers, n_out // tn),
        in_specs=[pl.BlockSpec((rows, D_MODEL), lambda l, j: (0, 0)),
                  pl.BlockSpec((1, D_MODEL, tn), lambda l, j: (l, 0, j)),
                  pl.BlockSpec((1, 1, tn), lambda l, j: (l, 0, j))],
        out_specs=pl.BlockSpec((1, rows, tn), lambda l, j: (l, 0, j)),
        out_shape=jax.ShapeDtypeStruct((n_layers, rows, n_out), F32),
        compiler_params=_cparams(("arbitrary", "arbitrary")),
        name="modulation",
    )(c_pad, w, b)


def _log_sigmoid(z):
    return jnp.minimum(z, 0.0) - jnp.log1p(jnp.exp(-jnp.abs(z)))


def _gla_pre_kernel(x_ref, sh_ref, sc_ref, gain_ref, w_ref, wa_ref, wa2_ref, ba_ref,
                    q_ref, k_ref, v_ref, r_ref, la_ref):
    h = _ada_norm(x_ref[...], gain_ref[...], sh_ref[0], sc_ref[0]).astype(BF16)
    y = _dot(h, w_ref[...])
    q_ref[...] = y[:, :GLA_DK] * (GLA_DK_H ** -0.5)
    k_ref[...] = y[:, GLA_DK:2 * GLA_DK]
    v_ref[...] = y[:, 2 * GLA_DK:2 * GLA_DK + GLA_DV].astype(BF16)
    r_ref[...] = y[:, 2 * GLA_DK + GLA_DV:]
    a = _dot(h, wa_ref[...])
    z = _dot(a.astype(BF16), wa2_ref[...]) + ba_ref[...]
    la_ref[...] = _log_sigmoid(z) * (1.0 / GLA_GATE_TAU)


def _gla_pre(x2, mod, n_slots, slot0, tm, tpb, gain, w_main, w_a, w_a2, b_a):
    n = x2.shape[0]
    row = lambda width: pl.BlockSpec((tm, width), lambda i: (i, 0))
    return pl.pallas_call(
        _gla_pre_kernel,
        grid=(n // tm,),
        in_specs=[row(D_MODEL), _mod_spec(slot0, n_slots, tpb), _mod_spec(slot0 + 1, n_slots, tpb),
                  _const_spec((1, D_MODEL)), _const_spec(w_main.shape), _const_spec(w_a.shape),
                  _const_spec(w_a2.shape), _const_spec((1, GLA_DK))],
        out_specs=[row(GLA_DK), row(GLA_DK), row(GLA_DV), row(GLA_DV), row(GLA_DK)],
        out_shape=[jax.ShapeDtypeStruct((n, GLA_DK), F32), jax.ShapeDtypeStruct((n, GLA_DK), F32),
                   jax.ShapeDtypeStruct((n, GLA_DV), BF16), jax.ShapeDtypeStruct((n, GLA_DV), F32),
                   jax.ShapeDtypeStruct((n, GLA_DK), F32)],
        compiler_params=_cparams(("arbitrary",)),
        name="gla_pre",
    )(x2, mod, mod, gain, w_main, w_a, w_a2, b_a)


def _gla_core_kernel(q_ref, k_ref, la_ref, v_ref, r_ref, gain_ref, s0_ref,
                     o_ref, st_ref, s_scr, *, chunk, n_chunks):
    j = pl.program_id(1)

    @pl.when(j == 0)
    def _():
        s_scr[...] = s0_ref[0]

    ri = lax.broadcasted_iota(jnp.int32, (chunk, chunk), 0)
    ci = lax.broadcasted_iota(jnp.int32, (chunk, chunk), 1)
    causal = ri >= ci
    tri = causal.astype(BF16)
    gain = gain_ref[...]

    def body(c, carry):
        rows = pl.ds(pl.multiple_of(c * chunk, chunk), chunk)
        g = la_ref[rows, :]
        g_hi = g.astype(BF16)
        g_r1 = g - g_hi.astype(F32)
        g_mid = g_r1.astype(BF16)
        g_lo = (g_r1 - g_mid.astype(F32)).astype(BF16)
        b = _dot(tri, g_hi) + _dot(tri, g_mid) + _dot(tri, g_lo)
        b_last = b[chunk - 1:chunk, :]
        q = q_ref[rows, :]
        k = k_ref[rows, :]
        q_dec = (q * jnp.exp(b)).astype(BF16)
        k_inv = (k * jnp.exp(-b)).astype(BF16)
        k_end = (k * jnp.exp(b_last - b)).astype(BF16)
        dec = jnp.exp(b_last)
        for h in range(GLA_HEADS):
            ks = slice(h * GLA_DK_H, (h + 1) * GLA_DK_H)
            vs = slice(h * GLA_DV_H, (h + 1) * GLA_DV_H)
            v_h = v_ref[rows, vs]
            s_t = s_scr[h]
            sc = jnp.where(causal, _dot_nt(q_dec[:, ks], k_inv[:, ks]), 0.0)
            o = _dot(sc.astype(BF16), v_h) + _dot_nt(q_dec[:, ks], s_t.astype(BF16))
            s_scr[h] = s_t * dec[:, ks] + _dot_tn(v_h, k_end[:, ks])
            on = o * lax.rsqrt(jnp.mean(o * o, axis=-1, keepdims=True) + EPS) * gain
            out = (on * _silu(r_ref[rows, vs])).astype(BF16)
            o_ref[0, 2 * h, rows, :] = out[:, :LANES]
            o_ref[0, 2 * h + 1, rows, :] = out[:, LANES:]
        return carry

    lax.fori_loop(0, n_chunks, body, 0)

    @pl.when(j == pl.num_programs(1) - 1)
    def _():
        st_ref[0] = s_scr[...]


def _gla_core(q, k, la, v, r, gain, s0_t, batch, seq, chunk, rows_per_step):
    n_chunks = rows_per_step // chunk
    steps = seq // rows_per_step
    row = lambda width: pl.BlockSpec((rows_per_step, width), lambda b, j: (b * steps + j, 0))
    state = pl.BlockSpec((1, GLA_HEADS, GLA_DV_H, GLA_DK_H), lambda b, j: (b, 0, 0, 0))
    return pl.pallas_call(
        functools.partial(_gla_core_kernel, chunk=chunk, n_chunks=n_chunks),
        grid=(batch, steps),
        in_specs=[row(GLA_DK), row(GLA_DK), row(GLA_DK), row(GLA_DV), row(GLA_DV),
                  _const_spec((1, GLA_DV_H)), state],
        out_specs=[pl.BlockSpec((1, HEAD_PAIRS, rows_per_step, LANES), lambda b, j: (b, 0, j, 0)),
                   state],
        out_shape=[jax.ShapeDtypeStruct((batch, HEAD_PAIRS, seq, LANES), BF16),
                   jax.ShapeDtypeStruct((batch, GLA_HEADS, GLA_DV_H, GLA_DK_H), F32)],
        scratch_shapes=[pltpu.VMEM((GLA_HEADS, GLA_DV_H, GLA_DK_H), F32)],
        compiler_params=_cparams(("arbitrary", "arbitrary")),
        name="gla_core",
    )(q, k, la, v, r, gain, s0_t)


def _top2_sum(a, b, c, d):
    hi1, lo1 = jnp.maximum(a, b), jnp.minimum(a, b)
    hi2, lo2 = jnp.maximum(c, d), jnp.minimum(c, d)
    return jnp.maximum(hi1, hi2) + jnp.maximum(jnp.minimum(hi1, hi2), jnp.maximum(lo1, lo2))


def _first_argmax(vals):
    idx = jnp.zeros(vals[0].shape, jnp.int32)
    best = vals[0]
    for j in range(1, len(vals)):
        upd = vals[j] > best
        idx = jnp.where(upd, j, idx)
        best = jnp.where(upd, vals[j], best)
    return idx


def _pick(idx, vals):
    out = vals[-1]
    for j in range(len(vals) - 2, -1, -1):
        out = jnp.where(idx == j, vals[j], out)
    return out


def _router_gates_t(logits_t, bias_col):
    s_t = jax.nn.sigmoid(logits_t)
    b_t = s_t + bias_col
    s = [s_t[e:e + 1, :] for e in range(N_EXPERTS)]
    b = [b_t[e:e + 1, :] for e in range(N_EXPERTS)]
    group_score = [_top2_sum(*b[4 * g:4 * g + 4]) for g in range(N_GROUPS)]
    best = _first_argmax(group_score)
    vb = [_pick(best, [b[4 * g + j] for g in range(N_GROUPS)]) for j in range(EXPERTS_PER_GROUP)]
    vs = [_pick(best, [s[4 * g + j] for g in range(N_GROUPS)]) for j in range(EXPERTS_PER_GROUP)]
    i1 = _first_argmax(vb)
    i2 = _first_argmax([jnp.where(i1 == j, -jnp.inf, vb[j]) for j in range(EXPERTS_PER_GROUP)])
    sel1 = _pick(i1, vs)
    sel2 = _pick(i2, vs)
    den = sel1 + sel2
    w1 = sel1 / den
    w2 = sel2 / den
    gates = []
    for g in range(N_GROUPS):
        for j in range(EXPERTS_PER_GROUP):
            val = jnp.where(i1 == j, w1, 0.0) + jnp.where(i2 == j, w2, 0.0)
            gates.append(jnp.where(best == g, val, 0.0))
    return gates


def _post_kernel(x_ref, y_ref, wo_ref, gm_ref, shf_ref, scf_ref, gf_ref, gain_ref,
                 wr_ref, rb_ref, wgu_ref, wd_ref, wsgu_ref, wsd_ref,
                 o_ref, xn_scr, h_scr, gate_scr, acc_scr):
    e = pl.program_id(1)
    tm = x_ref.shape[0]

    @pl.when(e == 0)
    def _():
        y = jnp.concatenate([y_ref[0, p] for p in range(HEAD_PAIRS)], axis=-1)
        xn = x_ref[...] + gm_ref[0] * _dot(y, wo_ref[...])
        xn_scr[...] = xn
        h = _ada_norm(xn, gain_ref[...], shf_ref[0], scf_ref[0]).astype(BF16)
        h_scr[...] = h
        gates = _router_gates_t(_dot_nt(wr_ref[...], h), rb_ref[...])
        gate_t = jnp.concatenate(gates + [jnp.zeros((LANES - N_EXPERTS, tm), F32)], axis=0)
        gate_scr[...] = gate_t.T
        sgu = _dot(h, wsgu_ref[...])
        act = _silu(sgu[:, :SHARED_FF]) * sgu[:, SHARED_FF:]
        acc_scr[...] = _dot(act.astype(BF16), wsd_ref[...])

    lane = lax.broadcasted_iota(jnp.int32, (tm, LANES), 1)
    gate = gate_scr[...]
    h = h_scr[...]
    acts = []
    for j in range(EXPERTS_PER_STEP):
        gu = _dot(h, wgu_ref[j])
        gcol = jnp.sum(jnp.where(lane == e * EXPERTS_PER_STEP + j, gate, 0.0), axis=-1, keepdims=True)
        acts.append((_silu(gu[:, :EXPERT_FF]) * gu[:, EXPERT_FF:] * gcol).astype(BF16))
    w_down = wd_ref[...].reshape(EXPERTS_PER_STEP * EXPERT_FF, D_MODEL)
    acc_scr[...] += _dot(jnp.concatenate(acts, axis=-1), w_down)

    @pl.when(e == N_EXPERTS // EXPERTS_PER_STEP - 1)
    def _():
        o_ref[...] = xn_scr[...] + gf_ref[0] * acc_scr[...]


def _post(x2, y, mod, n_slots, slot0, tm, tpb, w_o, gain_f, wr_t, rbias, wgu, wd, wsgu, wsd):
    n = x2.shape[0]
    ms = lambda s: pl.BlockSpec((1, 1, D_MODEL),
                                lambda i, e: ((i // tpb) * n_slots + slot0 + s, 0, 0))
    cs = lambda shape: pl.BlockSpec(shape, lambda i, e: (0,) * len(shape))
    return pl.pallas_call(
        _post_kernel,
        grid=(n // tm, N_EXPERTS // EXPERTS_PER_STEP),
        in_specs=[pl.BlockSpec((tm, D_MODEL), lambda i, e: (i, 0)),
                  pl.BlockSpec((1, HEAD_PAIRS, tm, LANES), lambda i, e: (i // tpb, 0, i % tpb, 0)),
                  cs((D_MODEL, D_MODEL)),
                  ms(2), ms(3), ms(4), ms(5), cs((1, D_MODEL)),
                  cs((N_EXPERTS, D_MODEL)), cs((N_EXPERTS, 1)),
                  pl.BlockSpec((EXPERTS_PER_STEP, D_MODEL, 2 * EXPERT_FF), lambda i, e: (e, 0, 0)),
                  pl.BlockSpec((EXPERTS_PER_STEP, EXPERT_FF, D_MODEL), lambda i, e: (e, 0, 0)),
                  cs((D_MODEL, 2 * SHARED_FF)), cs((SHARED_FF, D_MODEL))],
        out_specs=pl.BlockSpec((tm, D_MODEL), lambda i, e: (i, 0)),
        out_shape=jax.ShapeDtypeStruct((n, D_MODEL), F32),
        scratch_shapes=[pltpu.VMEM((tm, D_MODEL), F32), pltpu.VMEM((tm, D_MODEL), BF16),
                        pltpu.VMEM((tm, LANES), F32), pltpu.VMEM((tm, D_MODEL), F32)],
        compiler_params=_cparams(("arbitrary", "arbitrary")),
        name="post_moe",
    )(x2, y, w_o, mod, mod, mod, mod, gain_f, wr_t, rbias, wgu, wd, wsgu, wsd)


def _rope_block(x, cos, sin_a, sin_b):
    return (x * cos + pltpu.roll(x, LANES - ROPE_HALF, 1) * sin_a
            + pltpu.roll(x, ROPE_HALF, 1) * sin_b)


def _kv_latent_kernel(x_ref, sh_ref, sc_ref, gain_ref, wc_ref, wr_ref, cg_ref, rg_ref,
                      cos_ref, sa_ref, sb_ref, ckv_ref, kr_ref):
    h = _ada_norm(x_ref[...], gain_ref[...], sh_ref[0], sc_ref[0]).astype(BF16)
    c_raw = _dot(h, wc_ref[...])
    ckv_ref[...] = c_raw * lax.rsqrt(jnp.mean(c_raw * c_raw, axis=-1, keepdims=True) + EPS) * cg_ref[...]
    r_raw = _dot(h, wr_ref[...])
    ms = jnp.sum(r_raw * r_raw, axis=-1, keepdims=True) * (1.0 / ROPE_DIM)
    rn = r_raw * lax.rsqrt(ms + EPS) * rg_ref[...]
    kr_ref[...] = _rope_block(rn, cos_ref[...], sa_ref[...], sb_ref[...])


def _kv_latent(x2, mod, tm, tpb, gain, w_c, w_r, c_gain, r_gain, cos, sin_a, sin_b):
    n = x2.shape[0]
    row = lambda width: pl.BlockSpec((tm, width), lambda i: (i, 0))
    tab = pl.BlockSpec((tm, LANES), lambda i: (i % tpb, 0))
    return pl.pallas_call(
        _kv_latent_kernel,
        grid=(n // tm,),
        in_specs=[row(D_MODEL), _mod_spec(0, 2, tpb), _mod_spec(1, 2, tpb), _const_spec((1, D_MODEL)),
                  _const_spec((D_MODEL, KV_LORA)), _const_spec((D_MODEL, LANES)),
                  _const_spec((1, KV_LORA)), _const_spec((1, LANES)), tab, tab, tab],
        out_specs=[row(KV_LORA), row(LANES)],
        out_shape=[jax.ShapeDtypeStruct((n, KV_LORA), F32), jax.ShapeDtypeStruct((n, LANES), F32)],
        compiler_params=_cparams(("arbitrary",)),
        name="kv_latent",
    )(x2, mod, mod, gain, w_c, w_r, c_gain, r_gain, cos, sin_a, sin_b)


def _segment_inv_rms(x, seg_ref, exp_ref, inv_count):
    hi, lo = _split2(x * x)
    ss = _dot(hi, seg_ref[...]) + _dot(lo, seg_ref[...])
    inv = lax.rsqrt(ss * inv_count + EPS)
    ihi, ilo = _split2(inv)
    return _dot(ihi, exp_ref[...]) + _dot(ilo, exp_ref[...])


def _put_max_sq_norm(norms, blk, lane_idx):
    f = blk.astype(F32)
    n2 = jnp.max(jnp.sum(f * f, axis=-1, keepdims=True), axis=0, keepdims=True)
    lane = lax.broadcasted_iota(jnp.int32, norms.shape, 1)
    return jnp.where(lane == lane_idx, n2, norms)


def _accumulate_max(ref, value, first):
    @pl.when(first)
    def _():
        ref[0] = value

    @pl.when(jnp.logical_not(first))
    def _():
        ref[0] = jnp.maximum(ref[0], value)


def _kv_expand_kernel(c_ref, kr_ref, wk_ref, wvt_ref, seg_ref, exp_ref, cnt_ref, gl_ref, place_ref,
                      k_ref, vt_ref, n2_ref):
    c = c_ref[0].astype(BF16)
    kn = _dot(c, wk_ref[...])
    inv = _segment_inv_rms(kn, seg_ref, exp_ref, cnt_ref[...])
    k = kn * inv * gl_ref[...] + _dot(kr_ref[0].astype(BF16), place_ref[...])
    kb = k.astype(BF16)
    vt = _dot_nt(wvt_ref[...], c).astype(BF16)
    norms = jnp.zeros((1, LANES), F32)
    for h in range(MLA_HEADS):
        blk = kb[:, h * HEAD_BLOCK:(h + 1) * HEAD_BLOCK]
        k_ref[0, h] = blk
        norms = _put_max_sq_norm(norms, blk, h)
        vt_ref[0, h, :V_DIM, :] = vt[h * V_DIM:(h + 1) * V_DIM, :]
        vt_ref[0, h, V_DIM:, :] = jnp.ones((BF16_SUBLANES, vt.shape[1]), BF16)
    _accumulate_max(n2_ref, norms, pl.program_id(1) == 0)


def _kv_expand(ckv, kr, tm, w_k, w_vt, seg, exp, cnt, gain_lane, place):
    batch, lp, _ = ckv.shape
    width = MLA_HEADS * HEAD_BLOCK
    cs = lambda shape: pl.BlockSpec(shape, lambda b, i: (0,) * len(shape))
    return pl.pallas_call(
        _kv_expand_kernel,
        grid=(batch, lp // tm),
        in_specs=[pl.BlockSpec((1, tm, KV_LORA), lambda b, i: (b, i, 0)),
                  pl.BlockSpec((1, tm, LANES), lambda b, i: (b, i, 0)),
                  cs((KV_LORA, width)), cs((MLA_HEADS * V_DIM, KV_LORA)),
                  cs((width, LANES)), cs((LANES, width)), cs((1, LANES)), cs((1, width)),
                  cs((LANES, width))],
        out_specs=[pl.BlockSpec((1, MLA_HEADS, tm, HEAD_BLOCK), lambda b, i: (b, 0, i, 0)),
                   pl.BlockSpec((1, MLA_HEADS, V_AUG, tm), lambda b, i: (b, 0, 0, i)),
                   pl.BlockSpec((1, 1, LANES), lambda b, i: (b, 0, 0))],
        out_shape=[jax.ShapeDtypeStruct((batch, MLA_HEADS, lp, HEAD_BLOCK), BF16),
                   jax.ShapeDtypeStruct((batch, MLA_HEADS, V_AUG, lp), BF16),
                   jax.ShapeDtypeStruct((batch, 1, LANES), F32)],
        compiler_params=_cparams(("arbitrary", "arbitrary")),
        name="kv_expand",
    )(ckv, kr, w_k, w_vt, seg, exp, cnt, gain_lane, place)


def _mla_pre_kernel(x_ref, sh_ref, sc_ref, gain_ref, wdq_ref, qag_ref, wuq_ref, seg_ref, exp_ref,
                    cnt_ref, gl_ref, cos_ref, sa_ref, sb_ref, q_ref, n2_ref, *, tiles_per_batch):
    h = _ada_norm(x_ref[...], gain_ref[...], sh_ref[0], sc_ref[0]).astype(BF16)
    qa = _dot(h, wdq_ref[...])
    qa = qa * lax.rsqrt(jnp.mean(qa * qa, axis=-1, keepdims=True) + EPS) * qag_ref[...]
    q = _dot(qa.astype(BF16), wuq_ref[...])
    qn = q * _segment_inv_rms(q, seg_ref, exp_ref, cnt_ref[...]) * gl_ref[...]
    cos, sa, sb = cos_ref[...], sa_ref[...], sb_ref[...]
    norms = jnp.zeros((1, LANES), F32)
    for hd in range(MLA_HEADS):
        blk = _rope_block(qn[:, hd * HEAD_BLOCK:(hd + 1) * HEAD_BLOCK], cos, sa, sb).astype(BF16)
        q_ref[0, hd] = blk
        norms = _put_max_sq_norm(norms, blk, hd)
    _accumulate_max(n2_ref, norms, pl.program_id(0) % tiles_per_batch == 0)


def _mla_pre(x2, mod, n_slots, slot0, tm, tpb, batch, seq, gain, w_dq, qa_gain, w_uq, seg, exp, cnt,
             gain_lane, cos, sin_a, sin_b):
    n = x2.shape[0]
    width = MLA_HEADS * HEAD_BLOCK
    tab = pl.BlockSpec((tm, LANES), lambda i: (i % tpb, 0))
    return pl.pallas_call(
        functools.partial(_mla_pre_kernel, tiles_per_batch=tpb),
        grid=(n // tm,),
        in_specs=[pl.BlockSpec((tm, D_MODEL), lambda i: (i, 0)),
                  _mod_spec(slot0, n_slots, tpb), _mod_spec(slot0 + 1, n_slots, tpb),
                  _const_spec((1, D_MODEL)), _const_spec((D_MODEL, Q_LORA)), _const_spec((1, Q_LORA)),
                  _const_spec((Q_LORA, width)), _const_spec((width, LANES)), _const_spec((LANES, width)),
                  _const_spec((1, LANES)), _const_spec((1, width)), tab, tab, tab],
        out_specs=[pl.BlockSpec((1, MLA_HEADS, tm, HEAD_BLOCK), lambda i: (i // tpb, 0, i % tpb, 0)),
                   pl.BlockSpec((1, 1, LANES), lambda i: (i // tpb, 0, 0))],
        out_shape=[jax.ShapeDtypeStruct((batch, MLA_HEADS, seq, HEAD_BLOCK), BF16),
                   jax.ShapeDtypeStruct((batch, 1, LANES), F32)],
        compiler_params=_cparams(("arbitrary",)),
        name="mla_pre",
    )(x2, mod, mod, gain, w_dq, qa_gain, w_uq, seg, exp, cnt, gain_lane, cos, sin_a, sin_b)


def _flash_kernel(qi_ref, ki_ref, qp_ref, kp_ref, fl_ref, bounded_ref, q_ref, k_ref, vt_ref, o_ref,
                  m_scr, acc_scr, pa_scr, pb_scr, aa_scr, ab_scr, *, tq, tk, q_offset, n_keys):
    p = pl.program_id(1)
    q_lo = q_offset + qi_ref[p] * tq
    k_lo = ki_ref[p] * tk
    first_a = ki_ref[p] == 0
    first_b = (fl_ref[p] & 1) == 1
    last_b = (fl_ref[p] & 2) == 2

    @pl.when(p == 0)
    def _():
        pb_scr[...] = jnp.zeros(pb_scr.shape, BF16)
        ab_scr[...] = jnp.ones(ab_scr.shape, F32)
        acc_scr[...] = jnp.zeros(acc_scr.shape, F32)
        m_scr[...] = jnp.full(m_scr.shape, -jnp.inf, F32)

    def step(masked, shifted, p_w, p_r, a_w, a_r):
        def head(hd, carry):
            s = _dot_nt(k_ref[0, hd], q_ref[0, hd])
            pv = _dot(vt_ref[0, hd], p_r[hd])
            if masked:
                kidx = k_lo + lax.broadcasted_iota(jnp.int32, s.shape, 0)
                qpos = q_lo + lax.broadcasted_iota(jnp.int32, s.shape, 1)
                ok = ((jnp.right_shift(kidx, CHUNK_SHIFT) <= jnp.right_shift(qpos, CHUNK_SHIFT))
                      & (kidx < n_keys))
                s = jnp.where(ok, s, -jnp.inf)
            if shifted:
                m_old = jnp.where(first_a, -jnp.inf, m_scr[hd])
                m_new = jnp.maximum(m_old, jnp.max(s, axis=0, keepdims=True))
                m_scr[hd] = m_new
                a_w[hd] = jnp.exp2(m_old - m_new)
                p_w[hd] = jnp.exp2(s - m_new).astype(BF16)
                acc_scr[hd] = jnp.where(first_b, pv, acc_scr[hd] * a_r[hd] + pv)
            else:
                p_w[hd] = jnp.exp2(s).astype(BF16)
                acc_scr[hd] = jnp.where(first_b, pv, acc_scr[hd] + pv)
            return carry

        lax.fori_loop(0, MLA_HEADS, head, 0, unroll=1 if shifted else FLASH_HEAD_UNROLL)

    full = ((jnp.right_shift(k_lo + tk - 1, CHUNK_SHIFT) <= jnp.right_shift(q_lo, CHUNK_SHIFT))
            & (k_lo + tk <= n_keys))
    bounded = bounded_ref[pl.program_id(0)] == 1
    even = p % 2 == 0
    for masked in (False, True):
        for shifted in (False, True):
            for write_a in (False, True):
                cond = ((jnp.logical_not(full) if masked else full)
                        & (jnp.logical_not(bounded) if shifted else bounded)
                        & (even if write_a else jnp.logical_not(even)))
                bufs = (pa_scr, pb_scr, aa_scr, ab_scr) if write_a else (pb_scr, pa_scr, ab_scr, aa_scr)
                pl.when(cond)(functools.partial(step, masked, shifted, *bufs))

    @pl.when(last_b)
    def _():
        def fin(hp, carry):
            halves = []
            for hh in range(2):
                a = acc_scr[2 * hp + hh]
                halves.append(a[:V_DIM] * (1.0 / a[V_DIM:V_DIM + 1]))
            o_ref[0, hp] = jnp.concatenate(halves, axis=0).T.astype(BF16)
            return carry
        lax.fori_loop(0, HEAD_PAIRS, fin, 0)


def _flash_schedule(n_q, tq, tk, q_offset, n_keys, n_kt):
    pairs = []
    for qi in range(n_q):
        q_hi_chunk = (q_offset + qi * tq + tq - 1) // CHUNK
        k_max = min(n_keys, (q_hi_chunk + 1) * CHUNK)
        kt = min(n_kt, -(-k_max // tk))
        for ki in range(kt):
            pairs.append((qi, ki, (1 if ki == 0 else 0) | (2 if ki == kt - 1 else 0)))
    stage_a = pairs + [pairs[-1]]
    stage_b = [(pairs[0][0], pairs[0][1], 1)] + pairs
    cols = [[t[0] for t in stage_a], [t[1] for t in stage_a],
            [t[0] for t in stage_b], [t[1] for t in stage_b], [t[2] for t in stage_b]]
    return [jnp.asarray(np.array(c, np.int32)) for c in cols]


def _flash(q, k, vt, bounded, tq, tk, q_offset, n_keys):
    batch, _, seq, _ = q.shape
    lp = k.shape[2]
    sched = _flash_schedule(seq // tq, tq, tk, q_offset, n_keys, lp // tk)
    n_steps = int(sched[0].shape[0])
    grid_spec = pltpu.PrefetchScalarGridSpec(
        num_scalar_prefetch=6,
        grid=(batch, n_steps),
        in_specs=[pl.BlockSpec((1, MLA_HEADS, tq, HEAD_BLOCK),
                               lambda b, p, qi, ki, qp, kp, fl, bd: (b, 0, qi[p], 0)),
                  pl.BlockSpec((1, MLA_HEADS, tk, HEAD_BLOCK),
                               lambda b, p, qi, ki, qp, kp, fl, bd: (b, 0, ki[p], 0)),
                  pl.BlockSpec((1, MLA_HEADS, V_AUG, tk),
                               lambda b, p, qi, ki, qp, kp, fl, bd: (b, 0, 0, kp[p]))],
        out_specs=pl.BlockSpec((1, HEAD_PAIRS, tq, LANES),
                               lambda b, p, qi, ki, qp, kp, fl, bd: (b, 0, qp[p], 0)),
        scratch_shapes=[pltpu.VMEM((MLA_HEADS, 1, tq), F32), pltpu.VMEM((MLA_HEADS, V_AUG, tq), F32),
                        pltpu.VMEM((MLA_HEADS, tk, tq), BF16), pltpu.VMEM((MLA_HEADS, tk, tq), BF16),
                        pltpu.VMEM((MLA_HEADS, 1, tq), F32), pltpu.VMEM((MLA_HEADS, 1, tq), F32)],
    )
    return pl.pallas_call(
        functools.partial(_flash_kernel, tq=tq, tk=tk, q_offset=q_offset, n_keys=n_keys),
        grid_spec=grid_spec,
        out_shape=jax.ShapeDtypeStruct((batch, HEAD_PAIRS, seq, LANES), BF16),
        compiler_params=_cparams(("arbitrary", "arbitrary")),
        name="flash",
    )(*sched, bounded, q, k, vt)


def _head_block_tables():
    width = MLA_HEADS * HEAD_BLOCK
    seg_q = np.zeros((width, LANES), np.float32)
    seg_k = np.zeros((width, LANES), np.float32)
    place = np.zeros((LANES, width), np.float32)
    for h in range(MLA_HEADS):
        base = h * HEAD_BLOCK
        seg_q[base:base + NOPE_DIM, h] = 1.0
        seg_q[base + NOPE_DIM:base + NOPE_DIM + ROPE_DIM, MLA_HEADS + h] = 1.0
        seg_k[base:base + NOPE_DIM, h] = 1.0
        for d in range(ROPE_DIM):
            place[d, base + NOPE_DIM + d] = 1.0
    cnt_q = np.ones((1, LANES), np.float32)
    cnt_q[0, :MLA_HEADS] = 1.0 / NOPE_DIM
    cnt_q[0, MLA_HEADS:2 * MLA_HEADS] = 1.0 / ROPE_DIM
    cnt_k = np.ones((1, LANES), np.float32)
    cnt_k[0, :MLA_HEADS] = 1.0 / NOPE_DIM
    return seg_q, seg_k, place, cnt_q, cnt_k


def _to_head_blocks(w, per_head, parts):
    k_dim = w.shape[0]
    w3 = w.reshape(k_dim, MLA_HEADS, per_head)
    out = jnp.zeros((k_dim, MLA_HEADS, HEAD_BLOCK), w.dtype)
    for start, size, dest in parts:
        out = out.at[:, :, dest:dest + size].set(w3[:, :, start:start + size])
    return out.reshape(k_dim, MLA_HEADS * HEAD_BLOCK)


def _lane_gain(nope_gain, rope_gain, scale):
    blk = jnp.zeros((HEAD_BLOCK,), F32).at[:NOPE_DIM].set(nope_gain * scale)
    if rope_gain is not None:
        blk = blk.at[NOPE_DIM:NOPE_DIM + ROPE_DIM].set(rope_gain * scale)
    return jnp.tile(blk, MLA_HEADS).reshape(1, MLA_HEADS * HEAD_BLOCK)


def _rope_tables(positions, lane0):
    inv_freq = ROPE_BASE ** (-jnp.arange(ROPE_HALF, dtype=F32) / ROPE_HALF)
    ang = positions.astype(F32)[:, None] * inv_freq[None, :]
    cos, sin = jnp.cos(ang), jnp.sin(ang)
    t = positions.shape[0]
    z = jnp.zeros((t, LANES), F32)
    cos_t = z.at[:, lane0:lane0 + ROPE_HALF].set(cos).at[:, lane0 + ROPE_HALF:lane0 + ROPE_DIM].set(cos)
    if lane0 > 0:
        cos_t = cos_t.at[:, :lane0].set(1.0)
    sin_a = z.at[:, lane0:lane0 + ROPE_HALF].set(-sin)
    sin_b = z.at[:, lane0 + ROPE_HALF:lane0 + ROPE_DIM].set(sin)
    return cos_t, sin_a, sin_b


def _prep_weights(p):
    w = {}
    n_qkvr = 2 * GLA_DK + 2 * GLA_DV
    w["gla_main"] = p["gla_w_in"][:, :, :n_qkvr].astype(BF16)
    w["gla_a"] = jnp.pad(p["gla_w_in"][:, :, n_qkvr:], ((0, 0), (0, 0), (0, LANES - GLA_GATE_RANK))).astype(BF16)
    w["gla_a2"] = jnp.pad(p["gla_w_alpha2"], ((0, 0), (0, LANES - GLA_GATE_RANK), (0, 0))).astype(BF16)
    w["gla_ba"] = p["gla_b_alpha"].reshape(-1, 1, GLA_DK)
    w["gla_gain"] = p["gla_out_gain"].reshape(-1, 1, GLA_DV_H)
    w["gla_wo"] = p["gla_w_o"].astype(BF16)
    w["mla_wo"] = p["mla_w_o"].astype(BF16)
    w["wr_t"] = p["moe_w_router"].T.astype(BF16)
    w["rbias"] = p["moe_router_bias"].reshape(N_EXPERTS, 1)
    w["wgu"] = p["moe_w_gate_up"].astype(BF16)
    w["wd"] = p["moe_w_down"].astype(BF16)
    w["wsgu"] = p["moe_w_shared_gate_up"].astype(BF16)
    w["wsd"] = p["moe_w_shared_down"].astype(BF16)
    w["kv_wc"] = p["mla_w_dkv"][:, :KV_LORA].astype(BF16)
    w["kv_wr"] = jnp.pad(p["mla_w_dkv"][:, KV_LORA:], ((0, 0), (0, LANES - ROPE_DIM))).astype(BF16)
    w["ckv_gain"] = p["mla_ckv_gain"].reshape(1, KV_LORA)
    w["kr_gain"] = jnp.pad(p["mla_k_rope_gain"], (0, LANES - ROPE_DIM)).reshape(1, LANES)
    per_kv = NOPE_DIM + V_DIM
    w["w_uk"] = _to_head_blocks(p["mla_w_ukv"], per_kv, [(0, NOPE_DIM, 0)]).astype(BF16)
    w_uv = p["mla_w_ukv"].reshape(KV_LORA, MLA_HEADS, per_kv)[:, :, NOPE_DIM:]
    w["w_uv_t"] = w_uv.reshape(KV_LORA, MLA_HEADS * V_DIM).T.astype(BF16)
    per_q = NOPE_DIM + ROPE_DIM
    w["w_dq"] = p["mla_w_dq"].astype(BF16)
    w["w_uq"] = jnp.stack([
        _to_head_blocks(p["mla_w_uq"][j], per_q, [(0, per_q, 0)]) for j in range(p["mla_w_uq"].shape[0])
    ]).astype(BF16)
    scale = (NOPE_DIM + ROPE_DIM) ** -0.5 * LOG2_E
    w["q_lane_gain"] = jnp.stack([
        _lane_gain(p["mla_q_nope_gain"][j], p["mla_q_rope_gain"][j], scale)
        for j in range(p["mla_q_nope_gain"].shape[0])])
    w["k_lane_gain"] = _lane_gain(p["mla_k_nope_gain"], None, 1.0)
    seg_q, seg_k, place, cnt_q, cnt_k = _head_block_tables()
    w["seg_q"] = jnp.asarray(seg_q, BF16)
    w["exp_q"] = jnp.asarray(seg_q.T, BF16)
    w["seg_k"] = jnp.asarray(seg_k, BF16)
    w["exp_k"] = jnp.asarray(seg_k.T, BF16)
    w["place"] = jnp.asarray(place, BF16)
    w["cnt_q"] = jnp.asarray(cnt_q)
    w["cnt_k"] = jnp.asarray(cnt_k)
    return w


def _round_up(a, b):
    return -(-a // b) * b


def _trunk(x, mod_layers, mod_kv, gla_state, cache_ckv, cache_krope, p, w):
    batch, seq, _ = x.shape
    depth = mod_layers.shape[0]
    n_a = p["gla_w_in"].shape[0]
    n = batch * seq
    tm = min(512, seq)
    tpb = seq // tm
    tm_moe = min(1024, seq)
    x2 = x.reshape(n, D_MODEL)
    offset = 0 if cache_ckv is None else cache_ckv.shape[1]
    q_pos = offset + jnp.arange(seq, dtype=jnp.int32)
    kv_tabs = _rope_tables(q_pos, 0)
    q_tabs = _rope_tables(q_pos, NOPE_DIM)
    new_states = []
    ckv_new = krope_new = k_all = vt_all = k_n2 = None
    n_keys = 0
    tq = min(512, _round_up(seq, LANES))
    seq_q = _round_up(seq, tq)
    tk = 512
    for layer in range(depth):
        mod = mod_layers[layer].reshape(batch * 6, 1, D_MODEL)
        gain_m = p["norm_mix_gain"][layer].reshape(1, D_MODEL)
        gain_f = p["norm_ffn_gain"][layer].reshape(1, D_MODEL)
        if layer < n_a:
            q, k, v, r, la = _gla_pre(x2, mod, 6, 0, tm, tpb, gain_m, w["gla_main"][layer],
                                      w["gla_a"][layer], w["gla_a2"][layer], w["gla_ba"][layer])
            chunk = min(CHUNK, seq)
            rows_per_step = min(512, seq)
            s0_t = jnp.swapaxes(gla_state[layer], -1, -2)
            y, s_t = _gla_core(q, k, la, v, r, w["gla_gain"][layer], s0_t, batch, seq, chunk, rows_per_step)
            new_states.append(jnp.swapaxes(s_t, -1, -2))
            w_o = w["gla_wo"][layer]
        else:
            j = layer - n_a
            qh, q_n2 = _mla_pre(x2, mod, 6, 0, tm, tpb, batch, seq, gain_m, w["w_dq"][j],
                                p["mla_q_a_gain"][j].reshape(1, Q_LORA), w["w_uq"][j], w["seg_q"],
                                w["exp_q"], w["cnt_q"], w["q_lane_gain"][j], *q_tabs)
            if seq_q != seq:
                qh = jnp.pad(qh, ((0, 0), (0, 0), (0, seq_q - seq), (0, 0)))
            bounded = (jnp.max(q_n2 * k_n2, axis=(1, 2)) <= LOGIT_LIMIT ** 2).astype(jnp.int32)
            y = _flash(qh, k_all, vt_all, bounded, tq, tk, offset, n_keys)
            if seq_q != seq:
                y = y[:, :, :seq, :]
            w_o = w["mla_wo"][j]
        x2 = _post(x2, y, mod, 6, 0, tm_moe, seq // tm_moe, w_o, gain_f, w["wr_t"], w["rbias"],
                   w["wgu"][layer], w["wd"][layer], w["wsgu"][layer], w["wsd"][layer])
        if layer == n_a - 1:
            mkv = mod_kv.reshape(batch * 2, 1, D_MODEL)
            ckv2, kr2 = _kv_latent(x2, mkv, tm, tpb, p["kv_norm_gain"].reshape(1, D_MODEL), w["kv_wc"],
                                   w["kv_wr"], w["ckv_gain"], w["kr_gain"], *kv_tabs)
            ckv_new = ckv2.reshape(batch, seq, KV_LORA)
            kr_pad = kr2.reshape(batch, seq, LANES)
            krope_new = kr_pad[:, :, :ROPE_DIM]
            if cache_ckv is None:
                ckv_all, kr_all = ckv_new, kr_pad
            else:
                ckv_all = jnp.concatenate([cache_ckv, ckv_new], axis=1)
                kr_all = jnp.concatenate(
                    [jnp.pad(cache_krope, ((0, 0), (0, 0), (0, LANES - ROPE_DIM))), kr_pad], axis=1)
            n_keys = ckv_all.shape[1]
            lp = _round_up(n_keys, tk)
            if lp != n_keys:
                ckv_all = jnp.pad(ckv_all, ((0, 0), (0, lp - n_keys), (0, 0)))
                kr_all = jnp.pad(kr_all, ((0, 0), (0, lp - n_keys), (0, 0)))
            k_all, vt_all, k_n2 = _kv_expand(ckv_all, kr_all, tk, w["w_uk"], w["w_uv_t"], w["seg_k"], w["exp_k"],
                                       w["cnt_k"], w["k_lane_gain"], w["place"])
    return x2.reshape(batch, seq, D_MODEL), jnp.stack(new_states), ckv_new, krope_new


def kernel(x_prompt, x_sample, c_prompt, c_sample, state_gla, cache_kv_latent, cache_k_rope, w_ada, b_ada, norm_mix_gain, norm_ffn_gain, gla_w_in, gla_w_alpha2, gla_b_alpha, gla_out_gain, gla_w_o, kv_w_ada, kv_b_ada, kv_norm_gain, mla_w_dkv, mla_ckv_gain, mla_k_rope_gain, mla_w_ukv, mla_k_nope_gain, mla_w_dq, mla_q_a_gain, mla_w_uq, mla_q_nope_gain, mla_q_rope_gain, mla_w_o, moe_w_router, moe_router_bias, moe_w_gate_up, moe_w_down, moe_w_shared_gate_up, moe_w_shared_down):
    p = dict(norm_mix_gain=norm_mix_gain, norm_ffn_gain=norm_ffn_gain, gla_w_in=gla_w_in,
             gla_w_alpha2=gla_w_alpha2, gla_b_alpha=gla_b_alpha, gla_out_gain=gla_out_gain,
             gla_w_o=gla_w_o, kv_norm_gain=kv_norm_gain, mla_w_dkv=mla_w_dkv, mla_ckv_gain=mla_ckv_gain,
             mla_k_rope_gain=mla_k_rope_gain, mla_w_ukv=mla_w_ukv, mla_k_nope_gain=mla_k_nope_gain,
             mla_w_dq=mla_w_dq, mla_q_a_gain=mla_q_a_gain, mla_w_uq=mla_w_uq,
             mla_q_nope_gain=mla_q_nope_gain, mla_q_rope_gain=mla_q_rope_gain, mla_w_o=mla_w_o,
             moe_w_router=moe_w_router, moe_router_bias=moe_router_bias, moe_w_gate_up=moe_w_gate_up,
             moe_w_down=moe_w_down, moe_w_shared_gate_up=moe_w_shared_gate_up,
             moe_w_shared_down=moe_w_shared_down)
    w = _prep_weights(p)
    depth = w_ada.shape[0]
    bp, bs = c_prompt.shape[0], c_sample.shape[0]
    rows = _round_up(bp + bs, 8)
    c_all = jnp.concatenate([c_prompt, c_sample, jnp.zeros((rows - bp - bs, D_MODEL), F32)], axis=0)
    mod_l = _modulation(c_all, w_ada, b_ada.reshape(depth, 1, 6 * D_MODEL))
    mod_k = _modulation(c_all, kv_w_ada.reshape(1, D_MODEL, 2 * D_MODEL),
                        kv_b_ada.reshape(1, 1, 2 * D_MODEL))[0]
    gla_zero = jnp.zeros((state_gla.shape[0], bp) + state_gla.shape[2:], x_prompt.dtype)
    y_p, s_p, ckv_p, kr_p = _trunk(x_prompt, mod_l[:, :bp], mod_k[:bp], gla_zero, None, None, p, w)
    y_s, s_s, ckv_s, kr_s = _trunk(x_sample, mod_l[:, bp:bp + bs], mod_k[bp:bp + bs], state_gla,
                                   cache_kv_latent, cache_k_rope, p, w)
    return (y_p, y_s, s_p, ckv_p, kr_p, s_s, ckv_s, kr_s)
```

```python
import functools
import math

import numpy as np
import jax
import jax.numpy as jnp
from jax import lax
from jax.experimental import pallas as pl
from jax.experimental.pallas import tpu as pltpu

F32 = jnp.float32
BF16 = jnp.bfloat16

D_MODEL = 1024
CHUNK = 64
CHUNK_SHIFT = 6
GLA_HEADS = 4
GLA_DK = 512
GLA_DV = 1024
GLA_DK_H = 128
GLA_DV_H = 256
GLA_GATE_RANK = 16
GLA_GATE_TAU = 16.0
MLA_HEADS = 16
Q_LORA = 256
KV_LORA = 128
NOPE_DIM = 64
ROPE_DIM = 32
ROPE_HALF = 16
V_DIM = 64
ROPE_BASE = 10000.0
N_EXPERTS = 16
N_GROUPS = 4
EXPERTS_PER_GROUP = 4
EXPERT_FF = 256
SHARED_FF = 256
EPS = 1e-6

LANES = 128
HEAD_BLOCK = LANES
HEAD_PAIRS = MLA_HEADS // 2
BF16_SUBLANES = 16
V_AUG = V_DIM + BF16_SUBLANES
EXPERTS_PER_STEP = 2
MOE_ROW_TILE = 1024
LOG2_E = math.log2(math.e)
LOGIT_LIMIT = 40.0
NORM_SLACK = 1.05
VMEM_LIMIT = 56 * 1024 * 1024


def _cparams(sem):
    return pltpu.CompilerParams(dimension_semantics=sem, vmem_limit_bytes=VMEM_LIMIT)


def _dot(a, b):
    return jnp.dot(a, b, preferred_element_type=F32)


def _dot_nt(a, b):
    return lax.dot_general(a, b, (((1,), (1,)), ((), ())), preferred_element_type=F32)


def _dot_tn(a, b):
    return lax.dot_general(a, b, (((0,), (0,)), ((), ())), preferred_element_type=F32)


def _silu(x):
    return x * jax.nn.sigmoid(x)


def _split2(x):
    hi = x.astype(BF16)
    lo = (x - hi.astype(F32)).astype(BF16)
    return hi, lo


def _ada_norm(x, gain, shift, scale):
    y = x * lax.rsqrt(jnp.mean(x * x, axis=-1, keepdims=True) + EPS)
    return (y * gain) * (1.0 + scale) + shift


def _mod_value(ref):
    return ref[0] if len(ref.shape) == 3 else ref[...]


def _const_spec(shape):
    nd = len(shape)
    return pl.BlockSpec(shape, lambda *_: (0,) * nd)


def _mod_spec(slot, n_slots, tiles_per_batch):
    return pl.BlockSpec((1, 1, D_MODEL),
                        lambda i, *_: ((i // tiles_per_batch) * n_slots + slot, 0, 0))


def _mod_kernel(c_ref, w_ref, b_ref, o_ref):
    c = c_ref[...]
    o_ref[0] = _dot(_silu(c).astype(BF16), w_ref[0].astype(BF16)) + b_ref[0]


def _modulation(c_pad, w, b):
    n_layers, _, n_out = w.shape
    rows = c_pad.shape[0]
    tn = 1024
    return pl.pallas_call(
        _mod_kernel,
        grid=(n_layers, n_out // tn),
        in_specs=[pl.BlockSpec((rows, D_MODEL), lambda l, j: (0, 0)),
                  pl.BlockSpec((1, D_MODEL, tn), lambda l, j: (l, 0, j)),
                  pl.BlockSpec((1, 1, tn), lambda l, j: (l, 0, j))],
        out_specs=pl.BlockSpec((1, rows, tn), lambda l, j: (l, 0, j)),
        out_shape=jax.ShapeDtypeStruct((n_layers, rows, n_out), F32),
        compiler_params=_cparams(("arbitrary", "arbitrary")),
        name="modulation",
    )(c_pad, w, b)


def _log_sigmoid(z):
    return jnp.minimum(z, 0.0) - jnp.log1p(jnp.exp(-jnp.abs(z)))


def _gla_pre_kernel(x_ref, sh_ref, sc_ref, gain_ref, w_ref, wa_ref, wa2_ref, ba_ref,
                    q_ref, k_ref, v_ref, r_ref, la_ref):
    h = _ada_norm(x_ref[...], gain_ref[...], sh_ref[0], sc_ref[0]).astype(BF16)
    y = _dot(h, w_ref[...])
    q_ref[...] = y[:, :GLA_DK] * (GLA_DK_H ** -0.5)
    k_ref[...] = y[:, GLA_DK:2 * GLA_DK]
    v_ref[...] = y[:, 2 * GLA_DK:2 * GLA_DK + GLA_DV].astype(BF16)
    r_ref[...] = y[:, 2 * GLA_DK + GLA_DV:]
    a = _dot(h, wa_ref[...])
    z = _dot(a.astype(BF16), wa2_ref[...]) + ba_ref[...]
    la_ref[...] = _log_sigmoid(z) * (1.0 / GLA_GATE_TAU)


def _gla_pre(x2, mod, n_slots, slot0, tm, tpb, gain, w_main, w_a, w_a2, b_a):
    n = x2.shape[0]
    row = lambda width: pl.BlockSpec((tm, width), lambda i: (i, 0))
    return pl.pallas_call(
        _gla_pre_kernel,
        grid=(n // tm,),
        in_specs=[row(D_MODEL), _mod_spec(slot0, n_slots, tpb), _mod_spec(slot0 + 1, n_slots, tpb),
                  _const_spec((1, D_MODEL)), _const_spec(w_main.shape), _const_spec(w_a.shape),
                  _const_spec(w_a2.shape), _const_spec((1, GLA_DK))],
        out_specs=[row(GLA_DK), row(GLA_DK), row(GLA_DV), row(GLA_DV), row(GLA_DK)],
        out_shape=[jax.ShapeDtypeStruct((n, GLA_DK), F32), jax.ShapeDtypeStruct((n, GLA_DK), F32),
                   jax.ShapeDtypeStruct((n, GLA_DV), BF16), jax.ShapeDtypeStruct((n, GLA_DV), F32),
                   jax.ShapeDtypeStruct((n, GLA_DK), F32)],
        compiler_params=_cparams(("arbitrary",)),
        name="gla_pre",
    )(x2, mod, mod, gain, w_main, w_a, w_a2, b_a)


def _gla_core_kernel(q_ref, k_ref, la_ref, v_ref, r_ref, gain_ref, s0_ref,
                     o_ref, st_ref, s_scr, *, chunk, n_chunks):
    j = pl.program_id(1)

    @pl.when(j == 0)
    def _():
        s_scr[...] = s0_ref[0]

    ri = lax.broadcasted_iota(jnp.int32, (chunk, chunk), 0)
    ci = lax.broadcasted_iota(jnp.int32, (chunk, chunk), 1)
    causal = ri >= ci
    tri = causal.astype(BF16)
    gain = gain_ref[...]

    def body(c, carry):
        rows = pl.ds(pl.multiple_of(c * chunk, chunk), chunk)
        g = la_ref[rows, :]
        g_hi = g.astype(BF16)
        g_r1 = g - g_hi.astype(F32)
        g_mid = g_r1.astype(BF16)
        g_lo = (g_r1 - g_mid.astype(F32)).astype(BF16)
        b = _dot(tri, g_hi) + _dot(tri, g_mid) + _dot(tri, g_lo)
        b_last = b[chunk - 1:chunk, :]
        q = q_ref[rows, :]
        k = k_ref[rows, :]
        q_dec = (q * jnp.exp(b)).astype(BF16)
        k_inv = (k * jnp.exp(-b)).astype(BF16)
        k_end = (k * jnp.exp(b_last - b)).astype(BF16)
        dec = jnp.exp(b_last)
        for h in range(GLA_HEADS):
            ks = slice(h * GLA_DK_H, (h + 1) * GLA_DK_H)
            vs = slice(h * GLA_DV_H, (h + 1) * GLA_DV_H)
            v_h = v_ref[rows, vs]
            s_t = s_scr[h]
            sc = jnp.where(causal, _dot_nt(q_dec[:, ks], k_inv[:, ks]), 0.0)
            o = _dot(sc.astype(BF16), v_h) + _dot_nt(q_dec[:, ks], s_t.astype(BF16))
            s_scr[h] = s_t * dec[:, ks] + _dot_tn(v_h, k_end[:, ks])
            on = o * lax.rsqrt(jnp.mean(o * o, axis=-1, keepdims=True) + EPS) * gain
            out = (on * _silu(r_ref[rows, vs])).astype(BF16)
            o_ref[0, 2 * h, rows, :] = out[:, :LANES]
            o_ref[0, 2 * h + 1, rows, :] = out[:, LANES:]
        return carry

    lax.fori_loop(0, n_chunks, body, 0)

    @pl.when(j == pl.num_programs(1) - 1)
    def _():
        st_ref[0] = s_scr[...]


def _gla_core(q, k, la, v, r, gain, s0_t, batch, seq, chunk, rows_per_step):
    n_chunks = rows_per_step // chunk
    steps = seq // rows_per_step
    row = lambda width: pl.BlockSpec((rows_per_step, width), lambda b, j: (b * steps + j, 0))
    state = pl.BlockSpec((1, GLA_HEADS, GLA_DV_H, GLA_DK_H), lambda b, j: (b, 0, 0, 0))
    return pl.pallas_call(
        functools.partial(_gla_core_kernel, chunk=chunk, n_chunks=n_chunks),
        grid=(batch, steps),
        in_specs=[row(GLA_DK), row(GLA_DK), row(GLA_DK), row(GLA_DV), row(GLA_DV),
                  _const_spec((1, GLA_DV_H)), state],
        out_specs=[pl.BlockSpec((1, HEAD_PAIRS, rows_per_step, LANES), lambda b, j: (b, 0, j, 0)),
                   state],
        out_shape=[jax.ShapeDtypeStruct((batch, HEAD_PAIRS, seq, LANES), BF16),
                   jax.ShapeDtypeStruct((batch, GLA_HEADS, GLA_DV_H, GLA_DK_H), F32)],
        scratch_shapes=[pltpu.VMEM((GLA_HEADS, GLA_DV_H, GLA_DK_H), F32)],
        compiler_params=_cparams(("arbitrary", "arbitrary")),
        name="gla_core",
    )(q, k, la, v, r, gain, s0_t)


def _top2_sum(a, b, c, d):
    hi1, lo1 = jnp.maximum(a, b), jnp.minimum(a, b)
    hi2, lo2 = jnp.maximum(c, d), jnp.minimum(c, d)
    return jnp.maximum(hi1, hi2) + jnp.maximum(jnp.minimum(hi1, hi2), jnp.maximum(lo1, lo2))


def _first_argmax(vals):
    idx = jnp.zeros(vals[0].shape, jnp.int32)
    best = vals[0]
    for j in range(1, len(vals)):
        upd = vals[j] > best
        idx = jnp.where(upd, j, idx)
        best = jnp.where(upd, vals[j], best)
    return idx


def _pick(idx, vals):
    out = vals[-1]
    for j in range(len(vals) - 2, -1, -1):
        out = jnp.where(idx == j, vals[j], out)
    return out


def _router_gates_t(logits_t, bias_col):
    s_t = jax.nn.sigmoid(logits_t)
    b_t = s_t + bias_col
    s = [s_t[e:e + 1, :] for e in range(N_EXPERTS)]
    b = [b_t[e:e + 1, :] for e in range(N_EXPERTS)]
    group_score = [_top2_sum(*b[4 * g:4 * g + 4]) for g in range(N_GROUPS)]
    best = _first_argmax(group_score)
    vb = [_pick(best, [b[4 * g + j] for g in range(N_GROUPS)]) for j in range(EXPERTS_PER_GROUP)]
    vs = [_pick(best, [s[4 * g + j] for g in range(N_GROUPS)]) for j in range(EXPERTS_PER_GROUP)]
    i1 = _first_argmax(vb)
    i2 = _first_argmax([jnp.where(i1 == j, -jnp.inf, vb[j]) for j in range(EXPERTS_PER_GROUP)])
    sel1 = _pick(i1, vs)
    sel2 = _pick(i2, vs)
    den = sel1 + sel2
    w1 = sel1 / den
    w2 = sel2 / den
    gates = []
    for g in range(N_GROUPS):
        for j in range(EXPERTS_PER_GROUP):
            val = jnp.where(i1 == j, w1, 0.0) + jnp.where(i2 == j, w2, 0.0)
            gates.append(jnp.where(best == g, val, 0.0))
    return gates


def _post_kernel(x_ref, y_ref, wo_ref, gm_ref, shf_ref, scf_ref, gf_ref, gain_ref,
                 wr_ref, rb_ref, wgu_ref, wd_ref, wsgu_ref, wsd_ref,
                 o_ref, xn_scr, h_scr, gate_scr, acc_scr):
    e = pl.program_id(1)
    tm = x_ref.shape[0]

    @pl.when(e == 0)
    def _():
        y = jnp.concatenate([y_ref[:, p].reshape(tm, LANES) for p in range(HEAD_PAIRS)], axis=-1)
        xn = x_ref[...] + _mod_value(gm_ref) * _dot(y, wo_ref[...])
        xn_scr[...] = xn
        h = _ada_norm(xn, gain_ref[...], _mod_value(shf_ref), _mod_value(scf_ref)).astype(BF16)
        h_scr[...] = h
        gates = _router_gates_t(_dot_nt(wr_ref[...], h), rb_ref[...])
        gate_t = jnp.concatenate(gates + [jnp.zeros((LANES - N_EXPERTS, tm), F32)], axis=0)
        gate_scr[...] = gate_t.T
        sgu = _dot(h, wsgu_ref[...])
        act = _silu(sgu[:, :SHARED_FF]) * sgu[:, SHARED_FF:]
        acc_scr[...] = _dot(act.astype(BF16), wsd_ref[...])

    lane = lax.broadcasted_iota(jnp.int32, (tm, LANES), 1)
    gate = gate_scr[...]
    h = h_scr[...]
    acts = []
    for j in range(EXPERTS_PER_STEP):
        gu = _dot(h, wgu_ref[j])
        gcol = jnp.sum(jnp.where(lane == e * EXPERTS_PER_STEP + j, gate, 0.0), axis=-1, keepdims=True)
        acts.append((_silu(gu[:, :EXPERT_FF]) * gu[:, EXPERT_FF:] * gcol).astype(BF16))
    w_down = wd_ref[...].reshape(EXPERTS_PER_STEP * EXPERT_FF, D_MODEL)
    acc_scr[...] += _dot(jnp.concatenate(acts, axis=-1), w_down)

    @pl.when(e == N_EXPERTS // EXPERTS_PER_STEP - 1)
    def _():
        o_ref[...] = xn_scr[...] + _mod_value(gf_ref) * acc_scr[...]


def _post(x2, y, mod, n_slots, slot0, tm, tpb, w_o, gain_f, wr_t, rbias, wgu, wd, wsgu, wsd):
    n = x2.shape[0]
    batch, _, seq, _ = y.shape
    if tpb:
        ms = lambda s: pl.BlockSpec((1, 1, D_MODEL),
                                    lambda i, e: ((i // tpb) * n_slots + slot0 + s, 0, 0))
        y_spec = pl.BlockSpec((1, HEAD_PAIRS, tm, LANES), lambda i, e: (i // tpb, 0, i % tpb, 0))
    else:
        ms = lambda s: pl.BlockSpec((tm, D_MODEL), lambda i, e: (i, slot0 + s))
        y_spec = pl.BlockSpec((tm // seq, HEAD_PAIRS, seq, LANES), lambda i, e: (i, 0, 0, 0))
    cs = lambda shape: pl.BlockSpec(shape, lambda i, e: (0,) * len(shape))
    return pl.pallas_call(
        _post_kernel,
        grid=(n // tm, N_EXPERTS // EXPERTS_PER_STEP),
        in_specs=[pl.BlockSpec((tm, D_MODEL), lambda i, e: (i, 0)),
                  y_spec,
                  cs((D_MODEL, D_MODEL)),
                  ms(2), ms(3), ms(4), ms(5), cs((1, D_MODEL)),
                  cs((N_EXPERTS, D_MODEL)), cs((N_EXPERTS, 1)),
                  pl.BlockSpec((EXPERTS_PER_STEP, D_MODEL, 2 * EXPERT_FF), lambda i, e: (e, 0, 0)),
                  pl.BlockSpec((EXPERTS_PER_STEP, EXPERT_FF, D_MODEL), lambda i, e: (e, 0, 0)),
                  cs((D_MODEL, 2 * SHARED_FF)), cs((SHARED_FF, D_MODEL))],
        out_specs=pl.BlockSpec((tm, D_MODEL), lambda i, e: (i, 0)),
        out_shape=jax.ShapeDtypeStruct((n, D_MODEL), F32),
        scratch_shapes=[pltpu.VMEM((tm, D_MODEL), F32), pltpu.VMEM((tm, D_MODEL), BF16),
                        pltpu.VMEM((tm, LANES), F32), pltpu.VMEM((tm, D_MODEL), F32)],
        compiler_params=_cparams(("arbitrary", "arbitrary")),
        name="post_moe",
    )(x2, y, w_o, mod, mod, mod, mod, gain_f, wr_t, rbias, wgu, wd, wsgu, wsd)


def _rope_block(x, cos, sin_a, sin_b):
    return (x * cos + pltpu.roll(x, LANES - ROPE_HALF, 1) * sin_a
            + pltpu.roll(x, ROPE_HALF, 1) * sin_b)


def _kv_latent_kernel(x_ref, sh_ref, sc_ref, gain_ref, wc_ref, wr_ref, cg_ref, rg_ref,
                      cos_ref, sa_ref, sb_ref, ckv_ref, kr_ref):
    h = _ada_norm(x_ref[...], gain_ref[...], sh_ref[0], sc_ref[0]).astype(BF16)
    c_raw = _dot(h, wc_ref[...])
    ckv_ref[...] = c_raw * lax.rsqrt(jnp.mean(c_raw * c_raw, axis=-1, keepdims=True) + EPS) * cg_ref[...]
    r_raw = _dot(h, wr_ref[...])
    ms = jnp.sum(r_raw * r_raw, axis=-1, keepdims=True) * (1.0 / ROPE_DIM)
    rn = r_raw * lax.rsqrt(ms + EPS) * rg_ref[...]
    kr_ref[...] = _rope_block(rn, cos_ref[...], sa_ref[...], sb_ref[...])


def _kv_latent(x2, mod, tm, tpb, gain, w_c, w_r, c_gain, r_gain, cos, sin_a, sin_b):
    n = x2.shape[0]
    row = lambda width: pl.BlockSpec((tm, width), lambda i: (i, 0))
    tab = pl.BlockSpec((tm, LANES), lambda i: (i % tpb, 0))
    return pl.pallas_call(
        _kv_latent_kernel,
        grid=(n // tm,),
        in_specs=[row(D_MODEL), _mod_spec(0, 2, tpb), _mod_spec(1, 2, tpb), _const_spec((1, D_MODEL)),
                  _const_spec((D_MODEL, KV_LORA)), _const_spec((D_MODEL, LANES)),
                  _const_spec((1, KV_LORA)), _const_spec((1, LANES)), tab, tab, tab],
        out_specs=[row(KV_LORA), row(LANES)],
        out_shape=[jax.ShapeDtypeStruct((n, KV_LORA), F32), jax.ShapeDtypeStruct((n, LANES), F32)],
        compiler_params=_cparams(("arbitrary",)),
        name="kv_latent",
    )(x2, mod, mod, gain, w_c, w_r, c_gain, r_gain, cos, sin_a, sin_b)


def _segment_inv_rms(x, seg_ref, exp_ref, inv_count):
    hi, lo = _split2(x * x)
    ss = _dot(hi, seg_ref[...]) + _dot(lo, seg_ref[...])
    inv = lax.rsqrt(ss * inv_count + EPS)
    ihi, ilo = _split2(inv)
    return _dot(ihi, exp_ref[...]) + _dot(ilo, exp_ref[...])


def _max_head_sq_norm(x, head_seg_ref):
    n2 = _dot((x * x).astype(BF16), head_seg_ref[...])
    return jnp.max(n2, axis=0, keepdims=True)


def _accumulate_max(ref, value, first):
    @pl.when(first)
    def _():
        ref[0] = value

    @pl.when(jnp.logical_not(first))
    def _():
        ref[0] = jnp.maximum(ref[0], value)


def _kv_expand_kernel(c_ref, kr_ref, wk_ref, wvt_ref, seg_ref, exp_ref, cnt_ref, gl_ref, place_ref,
                      hseg_ref, k_ref, vt_ref, n2_ref):
    c = c_ref[0].astype(BF16)
    kn = _dot(c, wk_ref[...])
    inv = _segment_inv_rms(kn, seg_ref, exp_ref, cnt_ref[...])
    k = kn * inv * gl_ref[...] + _dot(kr_ref[0].astype(BF16), place_ref[...])
    kb = k.astype(BF16)
    vt = _dot_nt(wvt_ref[...], c).astype(BF16)
    for h in range(MLA_HEADS):
        k_ref[0, h] = kb[:, h * HEAD_BLOCK:(h + 1) * HEAD_BLOCK]
        vt_ref[0, 0, h, :V_DIM, :] = vt[h * V_DIM:(h + 1) * V_DIM, :]
        vt_ref[0, 0, h, V_DIM:, :] = jnp.ones((BF16_SUBLANES, vt.shape[1]), BF16)
    _accumulate_max(n2_ref, _max_head_sq_norm(k, hseg_ref), pl.program_id(1) == 0)


def _kv_expand(ckv, kr, tm, w_k, w_vt, seg, exp, cnt, gain_lane, place, head_seg):
    batch, lp, _ = ckv.shape
    width = MLA_HEADS * HEAD_BLOCK
    cs = lambda shape: pl.BlockSpec(shape, lambda b, i: (0,) * len(shape))
    return pl.pallas_call(
        _kv_expand_kernel,
        grid=(batch, lp // tm),
        in_specs=[pl.BlockSpec((1, tm, KV_LORA), lambda b, i: (b, i, 0)),
                  pl.BlockSpec((1, tm, LANES), lambda b, i: (b, i, 0)),
                  cs((KV_LORA, width)), cs((MLA_HEADS * V_DIM, KV_LORA)),
                  cs((width, LANES)), cs((LANES, width)), cs((1, LANES)), cs((1, width)),
                  cs((LANES, width)), cs((width, LANES))],
        out_specs=[pl.BlockSpec((1, MLA_HEADS, tm, HEAD_BLOCK), lambda b, i: (b, 0, i, 0)),
                   pl.BlockSpec((1, 1, MLA_HEADS, V_AUG, tm), lambda b, i: (b, i, 0, 0, 0)),
                   pl.BlockSpec((1, 1, LANES), lambda b, i: (b, 0, 0))],
        out_shape=[jax.ShapeDtypeStruct((batch, MLA_HEADS, lp, HEAD_BLOCK), BF16),
                   jax.ShapeDtypeStruct((batch, lp // tm, MLA_HEADS, V_AUG, tm), BF16),
                   jax.ShapeDtypeStruct((batch, 1, LANES), F32)],
        compiler_params=_cparams(("arbitrary", "arbitrary")),
        name="kv_expand",
    )(ckv, kr, w_k, w_vt, seg, exp, cnt, gain_lane, place, head_seg)


def _mla_pre_kernel(x_ref, sh_ref, sc_ref, gain_ref, wdq_ref, qag_ref, wuq_ref, seg_ref, exp_ref,
                    cnt_ref, gl_ref, hseg_ref, cos_ref, sa_ref, sb_ref, q_ref, n2_ref, *, tiles_per_batch):
    h = _ada_norm(x_ref[...], gain_ref[...], sh_ref[0], sc_ref[0]).astype(BF16)
    qa = _dot(h, wdq_ref[...])
    qa = qa * lax.rsqrt(jnp.mean(qa * qa, axis=-1, keepdims=True) + EPS) * qag_ref[...]
    q = _dot(qa.astype(BF16), wuq_ref[...])
    qn = q * _segment_inv_rms(q, seg_ref, exp_ref, cnt_ref[...]) * gl_ref[...]
    cos, sa, sb = cos_ref[...], sa_ref[...], sb_ref[...]
    for hd in range(MLA_HEADS):
        blk = qn[:, hd * HEAD_BLOCK:(hd + 1) * HEAD_BLOCK]
        q_ref[0, hd] = _rope_block(blk, cos, sa, sb).astype(BF16)
    _accumulate_max(n2_ref, _max_head_sq_norm(qn, hseg_ref), pl.program_id(0) % tiles_per_batch == 0)


def _mla_pre(x2, mod, n_slots, slot0, tm, tpb, batch, seq, gain, w_dq, qa_gain, w_uq, seg, exp, cnt,
             gain_lane, head_seg, cos, sin_a, sin_b):
    n = x2.shape[0]
    width = MLA_HEADS * HEAD_BLOCK
    tab = pl.BlockSpec((tm, LANES), lambda i: (i % tpb, 0))
    return pl.pallas_call(
        functools.partial(_mla_pre_kernel, tiles_per_batch=tpb),
        grid=(n // tm,),
        in_specs=[pl.BlockSpec((tm, D_MODEL), lambda i: (i, 0)),
                  _mod_spec(slot0, n_slots, tpb), _mod_spec(slot0 + 1, n_slots, tpb),
                  _const_spec((1, D_MODEL)), _const_spec((D_MODEL, Q_LORA)), _const_spec((1, Q_LORA)),
                  _const_spec((Q_LORA, width)), _const_spec((width, LANES)), _const_spec((LANES, width)),
                  _const_spec((1, LANES)), _const_spec((1, width)), _const_spec((width, LANES)),
                  tab, tab, tab],
        out_specs=[pl.BlockSpec((1, MLA_HEADS, tm, HEAD_BLOCK), lambda i: (i // tpb, 0, i % tpb, 0)),
                   pl.BlockSpec((1, 1, LANES), lambda i: (i // tpb, 0, 0))],
        out_shape=[jax.ShapeDtypeStruct((batch, MLA_HEADS, seq, HEAD_BLOCK), BF16),
                   jax.ShapeDtypeStruct((batch, 1, LANES), F32)],
        compiler_params=_cparams(("arbitrary",)),
        name="mla_pre",
    )(x2, mod, mod, gain, w_dq, qa_gain, w_uq, seg, exp, cnt, gain_lane, head_seg, cos, sin_a, sin_b)


def _flash_kernel(qi_ref, ki_ref, qp_ref, kp_ref, fl_ref, bounded_ref, q_ref, k_ref, vt_ref, o_ref,
                  m_scr, acc_scr, pa_scr, pb_scr, aa_scr, ab_scr, *, tq, tk, q_offset, n_keys):
    p = pl.program_id(1)
    q_lo = q_offset + qi_ref[p] * tq
    k_lo = ki_ref[p] * tk
    first_a = ki_ref[p] == 0
    first_b = (fl_ref[p] & 1) == 1
    last_b = (fl_ref[p] & 2) == 2

    @pl.when(p == 0)
    def _():
        pb_scr[...] = jnp.zeros(pb_scr.shape, BF16)
        ab_scr[...] = jnp.ones(ab_scr.shape, F32)
        acc_scr[...] = jnp.zeros(acc_scr.shape, F32)
        m_scr[...] = jnp.full(m_scr.shape, -jnp.inf, F32)

    def step(masked, shifted, p_w, p_r, a_w, a_r):
        def scores(hd):
            s = _dot_nt(k_ref[0, hd], q_ref[0, hd])
            if masked:
                kidx = k_lo + lax.broadcasted_iota(jnp.int32, s.shape, 0)
                qpos = q_lo + lax.broadcasted_iota(jnp.int32, s.shape, 1)
                ok = ((jnp.right_shift(kidx, CHUNK_SHIFT) <= jnp.right_shift(qpos, CHUNK_SHIFT))
                      & (kidx < n_keys))
                s = jnp.where(ok, s, -jnp.inf)
            return s

        if shifted:
            def head(hd, carry):
                s = scores(hd)
                pv = _dot(vt_ref[0, 0, hd], p_r[hd])
                m_old = jnp.where(first_a, -jnp.inf, m_scr[hd])
                m_new = jnp.maximum(m_old, jnp.max(s, axis=0, keepdims=True))
                m_scr[hd] = m_new
                a_w[hd] = jnp.exp2(m_old - m_new)
                p_w[hd] = jnp.exp2(s - m_new).astype(BF16)
                acc_scr[hd] = jnp.where(first_b, pv, acc_scr[hd] * a_r[hd] + pv)
                return carry

            lax.fori_loop(0, MLA_HEADS, head, 0)
        else:
            for hd in range(MLA_HEADS):
                p_w[hd] = jnp.exp2(scores(hd)).astype(BF16)
            for hd in range(MLA_HEADS):
                pv = _dot(vt_ref[0, 0, hd], p_r[hd])
                acc_scr[hd] = jnp.where(first_b, pv, acc_scr[hd] + pv)

    full = ((jnp.right_shift(k_lo + tk - 1, CHUNK_SHIFT) <= jnp.right_shift(q_lo, CHUNK_SHIFT))
            & (k_lo + tk <= n_keys))
    bounded = bounded_ref[pl.program_id(0)] == 1
    even = p % 2 == 0
    for masked in (False, True):
        for shifted in (False, True):
            for write_a in (False, True):
                cond = ((jnp.logical_not(full) if masked else full)
                        & (jnp.logical_not(bounded) if shifted else bounded)
                        & (even if write_a else jnp.logical_not(even)))
                bufs = (pa_scr, pb_scr, aa_scr, ab_scr) if write_a else (pb_scr, pa_scr, ab_scr, aa_scr)
                pl.when(cond)(functools.partial(step, masked, shifted, *bufs))

    @pl.when(last_b)
    def _():
        def fin(hp, carry):
            halves = []
            for hh in range(2):
                a = acc_scr[2 * hp + hh]
                halves.append(a[:V_DIM] * (1.0 / a[V_DIM:V_DIM + 1]))
            o_ref[0, hp] = jnp.concatenate(halves, axis=0).T.astype(BF16)
            return carry
        lax.fori_loop(0, HEAD_PAIRS, fin, 0)


def _flash_schedule(n_q, tq, tk, q_offset, n_keys, n_kt):
    pairs = []
    for qi in range(n_q):
        q_hi_chunk = (q_offset + qi * tq + tq - 1) // CHUNK
        k_max = min(n_keys, (q_hi_chunk + 1) * CHUNK)
        kt = min(n_kt, -(-k_max // tk))
        for ki in range(kt):
            pairs.append((qi, ki, (1 if ki == 0 else 0) | (2 if ki == kt - 1 else 0)))
    stage_a = pairs + [pairs[-1]]
    stage_b = [(pairs[0][0], pairs[0][1], 1)] + pairs
    cols = [[t[0] for t in stage_a], [t[1] for t in stage_a],
            [t[0] for t in stage_b], [t[1] for t in stage_b], [t[2] for t in stage_b]]
    return [jnp.asarray(np.array(c, np.int32)) for c in cols]


def _flash(q, k, vt, bounded, tq, tk, q_offset, n_keys):
    batch, _, seq, _ = q.shape
    lp = k.shape[2]
    sched = _flash_schedule(seq // tq, tq, tk, q_offset, n_keys, lp // tk)
    n_steps = int(sched[0].shape[0])
    grid_spec = pltpu.PrefetchScalarGridSpec(
        num_scalar_prefetch=6,
        grid=(batch, n_steps),
        in_specs=[pl.BlockSpec((1, MLA_HEADS, tq, HEAD_BLOCK),
                               lambda b, p, qi, ki, qp, kp, fl, bd: (b, 0, qi[p], 0)),
                  pl.BlockSpec((1, MLA_HEADS, tk, HEAD_BLOCK),
                               lambda b, p, qi, ki, qp, kp, fl, bd: (b, 0, ki[p], 0)),
                  pl.BlockSpec((1, 1, MLA_HEADS, V_AUG, tk),
                               lambda b, p, qi, ki, qp, kp, fl, bd: (b, kp[p], 0, 0, 0))],
        out_specs=pl.BlockSpec((1, HEAD_PAIRS, tq, LANES),
                               lambda b, p, qi, ki, qp, kp, fl, bd: (b, 0, qp[p], 0)),
        scratch_shapes=[pltpu.VMEM((MLA_HEADS, 1, tq), F32), pltpu.VMEM((MLA_HEADS, V_AUG, tq), F32),
                        pltpu.VMEM((MLA_HEADS, tk, tq), BF16), pltpu.VMEM((MLA_HEADS, tk, tq), BF16),
                        pltpu.VMEM((MLA_HEADS, 1, tq), F32), pltpu.VMEM((MLA_HEADS, 1, tq), F32)],
    )
    return pl.pallas_call(
        functools.partial(_flash_kernel, tq=tq, tk=tk, q_offset=q_offset, n_keys=n_keys),
        grid_spec=grid_spec,
        out_shape=jax.ShapeDtypeStruct((batch, HEAD_PAIRS, seq, LANES), BF16),
        compiler_params=_cparams(("arbitrary", "arbitrary")),
        name="flash",
    )(*sched, bounded, q, k, vt)


def _head_block_tables():
    width = MLA_HEADS * HEAD_BLOCK
    seg_q = np.zeros((width, LANES), np.float32)
    seg_k = np.zeros((width, LANES), np.float32)
    place = np.zeros((LANES, width), np.float32)
    for h in range(MLA_HEADS):
        base = h * HEAD_BLOCK
        seg_q[base:base + NOPE_DIM, h] = 1.0
        seg_q[base + NOPE_DIM:base + NOPE_DIM + ROPE_DIM, MLA_HEADS + h] = 1.0
        seg_k[base:base + NOPE_DIM, h] = 1.0
        for d in range(ROPE_DIM):
            place[d, base + NOPE_DIM + d] = 1.0
    cnt_q = np.ones((1, LANES), np.float32)
    cnt_q[0, :MLA_HEADS] = 1.0 / NOPE_DIM
    cnt_q[0, MLA_HEADS:2 * MLA_HEADS] = 1.0 / ROPE_DIM
    cnt_k = np.ones((1, LANES), np.float32)
    cnt_k[0, :MLA_HEADS] = 1.0 / NOPE_DIM
    head_seg = np.zeros((width, LANES), np.float32)
    for h in range(MLA_HEADS):
        head_seg[h * HEAD_BLOCK:(h + 1) * HEAD_BLOCK, h] = 1.0
    return seg_q, seg_k, place, cnt_q, cnt_k, head_seg


def _to_head_blocks(w, per_head, parts):
    k_dim = w.shape[0]
    w3 = w.reshape(k_dim, MLA_HEADS, per_head)
    out = jnp.zeros((k_dim, MLA_HEADS, HEAD_BLOCK), w.dtype)
    for start, size, dest in parts:
        out = out.at[:, :, dest:dest + size].set(w3[:, :, start:start + size])
    return out.reshape(k_dim, MLA_HEADS * HEAD_BLOCK)


def _lane_gain(nope_gain, rope_gain, scale):
    blk = jnp.zeros((HEAD_BLOCK,), F32).at[:NOPE_DIM].set(nope_gain * scale)
    if rope_gain is not None:
        blk = blk.at[NOPE_DIM:NOPE_DIM + ROPE_DIM].set(rope_gain * scale)
    return jnp.tile(blk, MLA_HEADS).reshape(1, MLA_HEADS * HEAD_BLOCK)


def _rope_tables(positions, lane0):
    inv_freq = ROPE_BASE ** (-jnp.arange(ROPE_HALF, dtype=F32) / ROPE_HALF)
    ang = positions.astype(F32)[:, None] * inv_freq[None, :]
    cos, sin = jnp.cos(ang), jnp.sin(ang)
    t = positions.shape[0]
    z = jnp.zeros((t, LANES), F32)
    cos_t = z.at[:, lane0:lane0 + ROPE_HALF].set(cos).at[:, lane0 + ROPE_HALF:lane0 + ROPE_DIM].set(cos)
    if lane0 > 0:
        cos_t = cos_t.at[:, :lane0].set(1.0)
    sin_a = z.at[:, lane0:lane0 + ROPE_HALF].set(-sin)
    sin_b = z.at[:, lane0 + ROPE_HALF:lane0 + ROPE_DIM].set(sin)
    return cos_t, sin_a, sin_b


def _prep_weights(p):
    w = {}
    n_qkvr = 2 * GLA_DK + 2 * GLA_DV
    w["gla_main"] = p["gla_w_in"][:, :, :n_qkvr].astype(BF16)
    w["gla_a"] = jnp.pad(p["gla_w_in"][:, :, n_qkvr:], ((0, 0), (0, 0), (0, LANES - GLA_GATE_RANK))).astype(BF16)
    w["gla_a2"] = jnp.pad(p["gla_w_alpha2"], ((0, 0), (0, LANES - GLA_GATE_RANK), (0, 0))).astype(BF16)
    w["gla_ba"] = p["gla_b_alpha"].reshape(-1, 1, GLA_DK)
    w["gla_gain"] = p["gla_out_gain"].reshape(-1, 1, GLA_DV_H)
    w["gla_wo"] = p["gla_w_o"].astype(BF16)
    w["mla_wo"] = p["mla_w_o"].astype(BF16)
    w["wr_t"] = p["moe_w_router"].T.astype(BF16)
    w["rbias"] = p["moe_router_bias"].reshape(N_EXPERTS, 1)
    w["wgu"] = p["moe_w_gate_up"].astype(BF16)
    w["wd"] = p["moe_w_down"].astype(BF16)
    w["wsgu"] = p["moe_w_shared_gate_up"].astype(BF16)
    w["wsd"] = p["moe_w_shared_down"].astype(BF16)
    w["kv_wc"] = p["mla_w_dkv"][:, :KV_LORA].astype(BF16)
    w["kv_wr"] = jnp.pad(p["mla_w_dkv"][:, KV_LORA:], ((0, 0), (0, LANES - ROPE_DIM))).astype(BF16)
    w["ckv_gain"] = p["mla_ckv_gain"].reshape(1, KV_LORA)
    w["kr_gain"] = jnp.pad(p["mla_k_rope_gain"], (0, LANES - ROPE_DIM)).reshape(1, LANES)
    per_kv = NOPE_DIM + V_DIM
    w["w_uk"] = _to_head_blocks(p["mla_w_ukv"], per_kv, [(0, NOPE_DIM, 0)]).astype(BF16)
    w_uv = p["mla_w_ukv"].reshape(KV_LORA, MLA_HEADS, per_kv)[:, :, NOPE_DIM:]
    w["w_uv_t"] = w_uv.reshape(KV_LORA, MLA_HEADS * V_DIM).T.astype(BF16)
    per_q = NOPE_DIM + ROPE_DIM
    w["w_dq"] = p["mla_w_dq"].astype(BF16)
    w["w_uq"] = jnp.stack([
        _to_head_blocks(p["mla_w_uq"][j], per_q, [(0, per_q, 0)]) for j in range(p["mla_w_uq"].shape[0])
    ]).astype(BF16)
    scale = (NOPE_DIM + ROPE_DIM) ** -0.5 * LOG2_E
    w["q_lane_gain"] = jnp.stack([
        _lane_gain(p["mla_q_nope_gain"][j], p["mla_q_rope_gain"][j], scale)
        for j in range(p["mla_q_nope_gain"].shape[0])])
    w["k_lane_gain"] = _lane_gain(p["mla_k_nope_gain"], None, 1.0)
    seg_q, seg_k, place, cnt_q, cnt_k, head_seg = _head_block_tables()
    w["head_seg"] = jnp.asarray(head_seg, BF16)
    w["seg_q"] = jnp.asarray(seg_q, BF16)
    w["exp_q"] = jnp.asarray(seg_q.T, BF16)
    w["seg_k"] = jnp.asarray(seg_k, BF16)
    w["exp_k"] = jnp.asarray(seg_k.T, BF16)
    w["place"] = jnp.asarray(place, BF16)
    w["cnt_q"] = jnp.asarray(cnt_q)
    w["cnt_k"] = jnp.asarray(cnt_k)
    return w


def _round_up(a, b):
    return -(-a // b) * b


def _trunk(x, mod_layers, mod_kv, gla_state, cache_ckv, cache_krope, p, w):
    batch, seq, _ = x.shape
    depth = mod_layers.shape[0]
    n_a = p["gla_w_in"].shape[0]
    n = batch * seq
    tm = min(512, seq)
    tpb = seq // tm
    tm_moe = min(MOE_ROW_TILE, seq)
    x2 = x.reshape(n, D_MODEL)
    offset = 0 if cache_ckv is None else cache_ckv.shape[1]
    q_pos = offset + jnp.arange(seq, dtype=jnp.int32)
    kv_tabs = _rope_tables(q_pos, 0)
    q_tabs = _rope_tables(q_pos, NOPE_DIM)
    new_states = []
    ckv_new = krope_new = k_all = vt_all = k_n2 = None
    n_keys = 0
    tq = min(512, _round_up(seq, LANES))
    seq_q = _round_up(seq, tq)
    tk = 512
    for layer in range(depth):
        mod = mod_layers[layer].reshape(batch * 6, 1, D_MODEL)
        gain_m = p["norm_mix_gain"][layer].reshape(1, D_MODEL)
        gain_f = p["norm_ffn_gain"][layer].reshape(1, D_MODEL)
        if layer < n_a:
            q, k, v, r, la = _gla_pre(x2, mod, 6, 0, tm, tpb, gain_m, w["gla_main"][layer],
                                      w["gla_a"][layer], w["gla_a2"][layer], w["gla_ba"][layer])
            chunk = min(CHUNK, seq)
            rows_per_step = min(512, seq)
            s0_t = jnp.swapaxes(gla_state[layer], -1, -2)
            y, s_t = _gla_core(q, k, la, v, r, w["gla_gain"][layer], s0_t, batch, seq, chunk, rows_per_step)
            new_states.append(jnp.swapaxes(s_t, -1, -2))
            w_o = w["gla_wo"][layer]
        else:
            j = layer - n_a
            qh, q_n2 = _mla_pre(x2, mod, 6, 0, tm, tpb, batch, seq, gain_m, w["w_dq"][j],
                                p["mla_q_a_gain"][j].reshape(1, Q_LORA), w["w_uq"][j], w["seg_q"],
                                w["exp_q"], w["cnt_q"], w["q_lane_gain"][j], w["head_seg"], *q_tabs)
            if seq_q != seq:
                qh = jnp.pad(qh, ((0, 0), (0, 0), (0, seq_q - seq), (0, 0)))
            bounded = (jnp.max(q_n2 * k_n2, axis=(1, 2)) * NORM_SLACK <= LOGIT_LIMIT ** 2).astype(jnp.int32)
            y = _flash(qh, k_all, vt_all, bounded, tq, tk, offset, n_keys)
            if seq_q != seq:
                y = y[:, :, :seq, :]
            w_o = w["mla_wo"][j]
        if n <= MOE_ROW_TILE and batch > 1:
            post_mod, post_tm, post_tpb = jnp.repeat(mod_layers[layer], seq, axis=0), n, 0
        else:
            post_mod, post_tm, post_tpb = mod, tm_moe, seq // tm_moe
        x2 = _post(x2, y, post_mod, 6, 0, post_tm, post_tpb, w_o, gain_f, w["wr_t"], w["rbias"],
                   w["wgu"][layer], w["wd"][layer], w["wsgu"][layer], w["wsd"][layer])
        if layer == n_a - 1:
            mkv = mod_kv.reshape(batch * 2, 1, D_MODEL)
            ckv2, kr2 = _kv_latent(x2, mkv, tm, tpb, p["kv_norm_gain"].reshape(1, D_MODEL), w["kv_wc"],
                                   w["kv_wr"], w["ckv_gain"], w["kr_gain"], *kv_tabs)
            ckv_new = ckv2.reshape(batch, seq, KV_LORA)
            kr_pad = kr2.reshape(batch, seq, LANES)
            krope_new = kr_pad[:, :, :ROPE_DIM]
            if cache_ckv is None:
                ckv_all, kr_all = ckv_new, kr_pad
            else:
                ckv_all = jnp.concatenate([cache_ckv, ckv_new], axis=1)
                kr_all = jnp.concatenate(
                    [jnp.pad(cache_krope, ((0, 0), (0, 0), (0, LANES - ROPE_DIM))), kr_pad], axis=1)
            n_keys = ckv_all.shape[1]
            lp = _round_up(n_keys, tk)
            if lp != n_keys:
                ckv_all = jnp.pad(ckv_all, ((0, 0), (0, lp - n_keys), (0, 0)))
                kr_all = jnp.pad(kr_all, ((0, 0), (0, lp - n_keys), (0, 0)))
            k_all, vt_all, k_n2 = _kv_expand(ckv_all, kr_all, tk, w["w_uk"], w["w_uv_t"], w["seg_k"], w["exp_k"],
                                       w["cnt_k"], w["k_lane_gain"], w["place"], w["head_seg"])
    return x2.reshape(batch, seq, D_MODEL), jnp.stack(new_states), ckv_new, krope_new


def kernel(x_prompt, x_sample, c_prompt, c_sample, state_gla, cache_kv_latent, cache_k_rope, w_ada, b_ada, norm_mix_gain, norm_ffn_gain, gla_w_in, gla_w_alpha2, gla_b_alpha, gla_out_gain, gla_w_o, kv_w_ada, kv_b_ada, kv_norm_gain, mla_w_dkv, mla_ckv_gain, mla_k_rope_gain, mla_w_ukv, mla_k_nope_gain, mla_w_dq, mla_q_a_gain, mla_w_uq, mla_q_nope_gain, mla_q_rope_gain, mla_w_o, moe_w_router, moe_router_bias, moe_w_gate_up, moe_w_down, moe_w_shared_gate_up, moe_w_shared_down):
    p = dict(norm_mix_gain=norm_mix_gain, norm_ffn_gain=norm_ffn_gain, gla_w_in=gla_w_in,
             gla_w_alpha2=gla_w_alpha2, gla_b_alpha=gla_b_alpha, gla_out_gain=gla_out_gain,
             gla_w_o=gla_w_o, kv_norm_gain=kv_norm_gain, mla_w_dkv=mla_w_dkv, mla_ckv_gain=mla_ckv_gain,
             mla_k_rope_gain=mla_k_rope_gain, mla_w_ukv=mla_w_ukv, mla_k_nope_gain=mla_k_nope_gain,
             mla_w_dq=mla_w_dq, mla_q_a_gain=mla_q_a_gain, mla_w_uq=mla_w_uq,
             mla_q_nope_gain=mla_q_nope_gain, mla_q_rope_gain=mla_q_rope_gain, mla_w_o=mla_w_o,
             moe_w_router=moe_w_router, moe_router_bias=moe_router_bias, moe_w_gate_up=moe_w_gate_up,
             moe_w_down=moe_w_down, moe_w_shared_gate_up=moe_w_shared_gate_up,
             moe_w_shared_down=moe_w_shared_down)
    w = _prep_weights(p)
    depth = w_ada.shape[0]
    bp, bs = c_prompt.shape[0], c_sample.shape[0]
    rows = _round_up(bp + bs, 8)
    c_all = jnp.concatenate([c_prompt, c_sample, jnp.zeros((rows - bp - bs, D_MODEL), F32)], axis=0)
    mod_l = _modulation(c_all, w_ada, b_ada.reshape(depth, 1, 6 * D_MODEL))
    mod_k = _modulation(c_all, kv_w_ada.reshape(1, D_MODEL, 2 * D_MODEL),
                        kv_b_ada.reshape(1, 1, 2 * D_MODEL))[0]
    gla_zero = jnp.zeros((state_gla.shape[0], bp) + state_gla.shape[2:], x_prompt.dtype)
    y_p, s_p, ckv_p, kr_p = _trunk(x_prompt, mod_l[:, :bp], mod_k[:bp], gla_zero, None, None, p, w)
    y_s, s_s, ckv_s, kr_s = _trunk(x_sample, mod_l[:, bp:bp + bs], mod_k[bp:bp + bs], state_gla,
                                   cache_kv_latent, cache_k_rope, p, w)
    return (y_p, y_s, s_p, ckv_p, kr_p, s_s, ckv_s, kr_s)
```

```python
import functools
import math

import numpy as np
import jax
import jax.numpy as jnp
from jax import lax
from jax.experimental import pallas as pl
from jax.experimental.pallas import tpu as pltpu

F32 = jnp.float32
BF16 = jnp.bfloat16

D_MODEL = 1024
CHUNK = 64
CHUNK_SHIFT = 6
GLA_HEADS = 4
GLA_DK = 512
GLA_DV = 1024
GLA_DK_H = 128
GLA_DV_H = 256
GLA_GATE_RANK = 16
GLA_GATE_TAU = 16.0
MLA_HEADS = 16
Q_LORA = 256
KV_LORA = 128
NOPE_DIM = 64
ROPE_DIM = 32
ROPE_HALF = 16
V_DIM = 64
ROPE_BASE = 10000.0
N_EXPERTS = 16
N_GROUPS = 4
EXPERTS_PER_GROUP = 4
EXPERT_FF = 256
SHARED_FF = 256
EPS = 1e-6

LANES = 128
HEAD_BLOCK = LANES
HEAD_PAIRS = MLA_HEADS // 2
BF16_SUBLANES = 16
V_AUG = V_DIM + BF16_SUBLANES
EXPERTS_PER_STEP = 4
MOE_ROW_TILE = 1024
LOG2_E = math.log2(math.e)
LOGIT_LIMIT = 40.0
NORM_SLACK = 1.05
VMEM_LIMIT = 56 * 1024 * 1024


def _cparams(sem):
    return pltpu.CompilerParams(dimension_semantics=sem, vmem_limit_bytes=VMEM_LIMIT)


def _dot(a, b):
    return jnp.dot(a, b, preferred_element_type=F32)


def _dot_nt(a, b):
    return lax.dot_general(a, b, (((1,), (1,)), ((), ())), preferred_element_type=F32)


def _dot_tn(a, b):
    return lax.dot_general(a, b, (((0,), (0,)), ((), ())), preferred_element_type=F32)


def _silu(x):
    return x * jax.nn.sigmoid(x)


def _split2(x):
    hi = x.astype(BF16)
    lo = (x - hi.astype(F32)).astype(BF16)
    return hi, lo


def _ada_norm(x, gain, shift, scale):
    y = x * lax.rsqrt(jnp.mean(x * x, axis=-1, keepdims=True) + EPS)
    return (y * gain) * (1.0 + scale) + shift


def _mod_value(ref):
    return ref[0] if len(ref.shape) == 3 else ref[...]


def _const_spec(shape):
    nd = len(shape)
    return pl.BlockSpec(shape, lambda *_: (0,) * nd)


def _mod_spec(slot, n_slots, tiles_per_batch):
    return pl.BlockSpec((1, 1, D_MODEL),
                        lambda i, *_: ((i // tiles_per_batch) * n_slots + slot, 0, 0))


def _mod_kernel(c_ref, w_ref, b_ref, o_ref):
    c = c_ref[...]
    o_ref[0] = _dot(_silu(c).astype(BF16), w_ref[0].astype(BF16)) + b_ref[0]


def _modulation(c_pad, w, b):
    n_layers, _, n_out = w.shape
    rows = c_pad.shape[0]
    tn = 1024
    return pl.pallas_call(
        _mod_kernel,
        grid=(n_layers, n_out // tn),
        in_specs=[pl.BlockSpec((rows, D_MODEL), lambda l, j: (0, 0)),
                  pl.BlockSpec((1, D_MODEL, tn), lambda l, j: (l, 0, j)),
                  pl.BlockSpec((1, 1, tn), lambda l, j: (l, 0, j))],
        out_specs=pl.BlockSpec((1, rows, tn), lambda l, j: (l, 0, j)),
        out_shape=jax.ShapeDtypeStruct((n_layers, rows, n_out), F32),
        compiler_params=_cparams(("arbitrary", "arbitrary")),
        name="modulation",
    )(c_pad, w, b)


def _log_sigmoid(z):
    return jnp.minimum(z, 0.0) - jnp.log1p(jnp.exp(-jnp.abs(z)))


def _gla_pre_kernel(x_ref, sh_ref, sc_ref, gain_ref, w_ref, wa_ref, wa2_ref, ba_ref,
                    q_ref, k_ref, v_ref, r_ref, la_ref):
    h = _ada_norm(x_ref[...], gain_ref[...], sh_ref[0], sc_ref[0]).astype(BF16)
    y = _dot(h, w_ref[...])
    q_ref[...] = y[:, :GLA_DK] * (GLA_DK_H ** -0.5)
    k_ref[...] = y[:, GLA_DK:2 * GLA_DK]
    v_ref[...] = y[:, 2 * GLA_DK:2 * GLA_DK + GLA_DV].astype(BF16)
    r_ref[...] = y[:, 2 * GLA_DK + GLA_DV:]
    a = _dot(h, wa_ref[...])
    z = _dot(a.astype(BF16), wa2_ref[...]) + ba_ref[...]
    la_ref[...] = _log_sigmoid(z) * (1.0 / GLA_GATE_TAU)


def _gla_pre(x2, mod, n_slots, slot0, tm, tpb, gain, w_main, w_a, w_a2, b_a):
    n = x2.shape[0]
    row = lambda width: pl.BlockSpec((tm, width), lambda i: (i, 0))
    return pl.pallas_call(
        _gla_pre_kernel,
        grid=(n // tm,),
        in_specs=[row(D_MODEL), _mod_spec(slot0, n_slots, tpb), _mod_spec(slot0 + 1, n_slots, tpb),
                  _const_spec((1, D_MODEL)), _const_spec(w_main.shape), _const_spec(w_a.shape),
                  _const_spec(w_a2.shape), _const_spec((1, GLA_DK))],
        out_specs=[row(GLA_DK), row(GLA_DK), row(GLA_DV), row(GLA_DV), row(GLA_DK)],
        out_shape=[jax.ShapeDtypeStruct((n, GLA_DK), F32), jax.ShapeDtypeStruct((n, GLA_DK), F32),
                   jax.ShapeDtypeStruct((n, GLA_DV), BF16), jax.ShapeDtypeStruct((n, GLA_DV), F32),
                   jax.ShapeDtypeStruct((n, GLA_DK), F32)],
        compiler_params=_cparams(("arbitrary",)),
        name="gla_pre",
    )(x2, mod, mod, gain, w_main, w_a, w_a2, b_a)


def _gla_core_kernel(q_ref, k_ref, la_ref, v_ref, r_ref, gain_ref, s0_ref,
                     o_ref, st_ref, s_scr, *, chunk, n_chunks):
    j = pl.program_id(0)
    batch = q_ref.shape[0]

    @pl.when(j == 0)
    def _():
        s_scr[...] = s0_ref[...]

    ri = lax.broadcasted_iota(jnp.int32, (chunk, chunk), 0)
    ci = lax.broadcasted_iota(jnp.int32, (chunk, chunk), 1)
    causal = ri >= ci
    tri = causal.astype(BF16)
    gain = gain_ref[...]

    def body(c, carry):
        rows = pl.ds(pl.multiple_of(c * chunk, chunk), chunk)
        for bi in range(batch):
            g = la_ref[bi, rows, :]
            g_hi = g.astype(BF16)
            g_r1 = g - g_hi.astype(F32)
            g_mid = g_r1.astype(BF16)
            g_lo = (g_r1 - g_mid.astype(F32)).astype(BF16)
            b = _dot(tri, g_hi) + _dot(tri, g_mid) + _dot(tri, g_lo)
            b_last = b[chunk - 1:chunk, :]
            q = q_ref[bi, rows, :]
            k = k_ref[bi, rows, :]
            q_dec = (q * jnp.exp(b)).astype(BF16)
            k_inv = (k * jnp.exp(-b)).astype(BF16)
            k_end = (k * jnp.exp(b_last - b)).astype(BF16)
            dec = jnp.exp(b_last)
            for h in range(GLA_HEADS):
                ks = slice(h * GLA_DK_H, (h + 1) * GLA_DK_H)
                vs = slice(h * GLA_DV_H, (h + 1) * GLA_DV_H)
                v_h = v_ref[bi, rows, vs]
                s_t = s_scr[bi, h]
                sc = jnp.where(causal, _dot_nt(q_dec[:, ks], k_inv[:, ks]), 0.0)
                o = _dot(sc.astype(BF16), v_h) + _dot_nt(q_dec[:, ks], s_t.astype(BF16))
                s_scr[bi, h] = s_t * dec[:, ks] + _dot_tn(v_h, k_end[:, ks])
                on = o * lax.rsqrt(jnp.mean(o * o, axis=-1, keepdims=True) + EPS) * gain
                out = (on * _silu(r_ref[bi, rows, vs])).astype(BF16)
                o_ref[bi, 2 * h, rows, :] = out[:, :LANES]
                o_ref[bi, 2 * h + 1, rows, :] = out[:, LANES:]
        return carry

    lax.fori_loop(0, n_chunks, body, 0)

    @pl.when(j == pl.num_programs(0) - 1)
    def _():
        st_ref[...] = s_scr[...]


def _gla_core(q, k, la, v, r, gain, s0_t, batch, seq, chunk, rows_per_step):
    n_chunks = rows_per_step // chunk
    row = lambda width: pl.BlockSpec((batch, rows_per_step, width), lambda j: (0, j, 0))
    state = pl.BlockSpec((batch, GLA_HEADS, GLA_DV_H, GLA_DK_H), lambda j: (0, 0, 0, 0))
    as3d = lambda a: a.reshape(batch, seq, a.shape[-1])
    return pl.pallas_call(
        functools.partial(_gla_core_kernel, chunk=chunk, n_chunks=n_chunks),
        grid=(seq // rows_per_step,),
        in_specs=[row(GLA_DK), row(GLA_DK), row(GLA_DK), row(GLA_DV), row(GLA_DV),
                  _const_spec((1, GLA_DV_H)), state],
        out_specs=[pl.BlockSpec((batch, HEAD_PAIRS, rows_per_step, LANES), lambda j: (0, 0, j, 0)),
                   state],
        out_shape=[jax.ShapeDtypeStruct((batch, HEAD_PAIRS, seq, LANES), BF16),
                   jax.ShapeDtypeStruct((batch, GLA_HEADS, GLA_DV_H, GLA_DK_H), F32)],
        scratch_shapes=[pltpu.VMEM((batch, GLA_HEADS, GLA_DV_H, GLA_DK_H), F32)],
        compiler_params=_cparams(("arbitrary",)),
        name="gla_core",
    )(as3d(q), as3d(k), as3d(la), as3d(v), as3d(r), gain, s0_t)


def _top2_sum(a, b, c, d):
    hi1, lo1 = jnp.maximum(a, b), jnp.minimum(a, b)
    hi2, lo2 = jnp.maximum(c, d), jnp.minimum(c, d)
    return jnp.maximum(hi1, hi2) + jnp.maximum(jnp.minimum(hi1, hi2), jnp.maximum(lo1, lo2))


def _first_argmax(vals):
    idx = jnp.zeros(vals[0].shape, jnp.int32)
    best = vals[0]
    for j in range(1, len(vals)):
        upd = vals[j] > best
        idx = jnp.where(upd, j, idx)
        best = jnp.where(upd, vals[j], best)
    return idx


def _pick(idx, vals):
    out = vals[-1]
    for j in range(len(vals) - 2, -1, -1):
        out = jnp.where(idx == j, vals[j], out)
    return out


def _router_gates_t(logits_t, bias_col):
    s_t = jax.nn.sigmoid(logits_t)
    b_t = s_t + bias_col
    s = [s_t[e:e + 1, :] for e in range(N_EXPERTS)]
    b = [b_t[e:e + 1, :] for e in range(N_EXPERTS)]
    group_score = [_top2_sum(*b[4 * g:4 * g + 4]) for g in range(N_GROUPS)]
    best = _first_argmax(group_score)
    vb = [_pick(best, [b[4 * g + j] for g in range(N_GROUPS)]) for j in range(EXPERTS_PER_GROUP)]
    vs = [_pick(best, [s[4 * g + j] for g in range(N_GROUPS)]) for j in range(EXPERTS_PER_GROUP)]
    i1 = _first_argmax(vb)
    i2 = _first_argmax([jnp.where(i1 == j, -jnp.inf, vb[j]) for j in range(EXPERTS_PER_GROUP)])
    sel1 = _pick(i1, vs)
    sel2 = _pick(i2, vs)
    den = sel1 + sel2
    w1 = sel1 / den
    w2 = sel2 / den
    gates = []
    for g in range(N_GROUPS):
        for j in range(EXPERTS_PER_GROUP):
            val = jnp.where(i1 == j, w1, 0.0) + jnp.where(i2 == j, w2, 0.0)
            gates.append(jnp.where(best == g, val, 0.0))
    return gates


def _post_kernel(x_ref, y_ref, wo_ref, gm_ref, shf_ref, scf_ref, gf_ref, gain_ref,
                 wr_ref, rb_ref, wgu_ref, wd_ref, wsgu_ref, wsd_ref,
                 o_ref, xn_scr, h_scr, gate_scr, acc_scr):
    e = pl.program_id(1)
    tm = x_ref.shape[0]

    @pl.when(e == 0)
    def _():
        y = jnp.concatenate([y_ref[:, p].reshape(tm, LANES) for p in range(HEAD_PAIRS)], axis=-1)
        xn = x_ref[...] + _mod_value(gm_ref) * _dot(y, wo_ref[...])
        xn_scr[...] = xn
        h = _ada_norm(xn, gain_ref[...], _mod_value(shf_ref), _mod_value(scf_ref)).astype(BF16)
        h_scr[...] = h
        gates = _router_gates_t(_dot_nt(wr_ref[...], h), rb_ref[...])
        gate_t = jnp.concatenate(gates + [jnp.zeros((LANES - N_EXPERTS, tm), F32)], axis=0)
        gate_scr[...] = gate_t.T
        sgu = _dot(h, wsgu_ref[...])
        act = _silu(sgu[:, :SHARED_FF]) * sgu[:, SHARED_FF:]
        acc_scr[...] = _dot(act.astype(BF16), wsd_ref[...])

    lane = lax.broadcasted_iota(jnp.int32, (tm, LANES), 1)
    gate = gate_scr[...]
    h = h_scr[...]
    acts = []
    for j in range(EXPERTS_PER_STEP):
        gu = _dot(h, wgu_ref[j])
        gcol = jnp.sum(jnp.where(lane == e * EXPERTS_PER_STEP + j, gate, 0.0), axis=-1, keepdims=True)
        acts.append((_silu(gu[:, :EXPERT_FF]) * gu[:, EXPERT_FF:] * gcol).astype(BF16))
    w_down = wd_ref[...].reshape(EXPERTS_PER_STEP * EXPERT_FF, D_MODEL)
    acc_scr[...] += _dot(jnp.concatenate(acts, axis=-1), w_down)

    @pl.when(e == N_EXPERTS // EXPERTS_PER_STEP - 1)
    def _():
        o_ref[...] = xn_scr[...] + _mod_value(gf_ref) * acc_scr[...]


def _post(x2, y, mod, n_slots, slot0, tm, tpb, w_o, gain_f, wr_t, rbias, wgu, wd, wsgu, wsd):
    n = x2.shape[0]
    batch, _, seq, _ = y.shape
    if tpb:
        ms = lambda s: pl.BlockSpec((1, 1, D_MODEL),
                                    lambda i, e: ((i // tpb) * n_slots + slot0 + s, 0, 0))
        y_spec = pl.BlockSpec((1, HEAD_PAIRS, tm, LANES), lambda i, e: (i // tpb, 0, i % tpb, 0))
    else:
        ms = lambda s: pl.BlockSpec((tm, D_MODEL), lambda i, e: (i, slot0 + s))
        y_spec = pl.BlockSpec((tm // seq, HEAD_PAIRS, seq, LANES), lambda i, e: (i, 0, 0, 0))
    cs = lambda shape: pl.BlockSpec(shape, lambda i, e: (0,) * len(shape))
    return pl.pallas_call(
        _post_kernel,
        grid=(n // tm, N_EXPERTS // EXPERTS_PER_STEP),
        in_specs=[pl.BlockSpec((tm, D_MODEL), lambda i, e: (i, 0)),
                  y_spec,
                  cs((D_MODEL, D_MODEL)),
                  ms(2), ms(3), ms(4), ms(5), cs((1, D_MODEL)),
                  cs((N_EXPERTS, D_MODEL)), cs((N_EXPERTS, 1)),
                  pl.BlockSpec((EXPERTS_PER_STEP, D_MODEL, 2 * EXPERT_FF), lambda i, e: (e, 0, 0)),
                  pl.BlockSpec((EXPERTS_PER_STEP, EXPERT_FF, D_MODEL), lambda i, e: (e, 0, 0)),
                  cs((D_MODEL, 2 * SHARED_FF)), cs((SHARED_FF, D_MODEL))],
        out_specs=pl.BlockSpec((tm, D_MODEL), lambda i, e: (i, 0)),
        out_shape=jax.ShapeDtypeStruct((n, D_MODEL), F32),
        scratch_shapes=[pltpu.VMEM((tm, D_MODEL), F32), pltpu.VMEM((tm, D_MODEL), BF16),
                        pltpu.VMEM((tm, LANES), F32), pltpu.VMEM((tm, D_MODEL), F32)],
        compiler_params=_cparams(("arbitrary", "arbitrary")),
        name="post_moe",
    )(x2, y, w_o, mod, mod, mod, mod, gain_f, wr_t, rbias, wgu, wd, wsgu, wsd)


def _rope_block(x, cos, sin_a, sin_b):
    return (x * cos + pltpu.roll(x, LANES - ROPE_HALF, 1) * sin_a
            + pltpu.roll(x, ROPE_HALF, 1) * sin_b)


def _kv_latent_kernel(x_ref, sh_ref, sc_ref, gain_ref, wc_ref, wr_ref, cg_ref, rg_ref,
                      cos_ref, sa_ref, sb_ref, ckv_ref, kr_ref):
    h = _ada_norm(x_ref[...], gain_ref[...], sh_ref[0], sc_ref[0]).astype(BF16)
    c_raw = _dot(h, wc_ref[...])
    ckv_ref[...] = c_raw * lax.rsqrt(jnp.mean(c_raw * c_raw, axis=-1, keepdims=True) + EPS) * cg_ref[...]
    r_raw = _dot(h, wr_ref[...])
    ms = jnp.sum(r_raw * r_raw, axis=-1, keepdims=True) * (1.0 / ROPE_DIM)
    rn = r_raw * lax.rsqrt(ms + EPS) * rg_ref[...]
    kr_ref[...] = _rope_block(rn, cos_ref[...], sa_ref[...], sb_ref[...])


def _kv_latent(x2, mod, tm, tpb, gain, w_c, w_r, c_gain, r_gain, cos, sin_a, sin_b):
    n = x2.shape[0]
    row = lambda width: pl.BlockSpec((tm, width), lambda i: (i, 0))
    tab = pl.BlockSpec((tm, LANES), lambda i: (i % tpb, 0))
    return pl.pallas_call(
        _kv_latent_kernel,
        grid=(n // tm,),
        in_specs=[row(D_MODEL), _mod_spec(0, 2, tpb), _mod_spec(1, 2, tpb), _const_spec((1, D_MODEL)),
                  _const_spec((D_MODEL, KV_LORA)), _const_spec((D_MODEL, LANES)),
                  _const_spec((1, KV_LORA)), _const_spec((1, LANES)), tab, tab, tab],
        out_specs=[row(KV_LORA), row(LANES)],
        out_shape=[jax.ShapeDtypeStruct((n, KV_LORA), F32), jax.ShapeDtypeStruct((n, LANES), F32)],
        compiler_params=_cparams(("arbitrary",)),
        name="kv_latent",
    )(x2, mod, mod, gain, w_c, w_r, c_gain, r_gain, cos, sin_a, sin_b)


def _segment_inv_rms(x, seg_ref, exp_ref, inv_count):
    ss = _dot((x * x).astype(BF16), seg_ref[...])
    inv = lax.rsqrt(ss * inv_count + EPS)
    ihi, ilo = _split2(inv)
    return _dot(ihi, exp_ref[...]) + _dot(ilo, exp_ref[...])


def _max_head_sq_norm(x, head_seg_ref):
    n2 = _dot((x * x).astype(BF16), head_seg_ref[...])
    return jnp.max(n2, axis=0, keepdims=True)


def _accumulate_max(ref, value, first):
    @pl.when(first)
    def _():
        ref[0] = value

    @pl.when(jnp.logical_not(first))
    def _():
        ref[0] = jnp.maximum(ref[0], value)


def _kv_expand_kernel(c_ref, kr_ref, wk_ref, wvt_ref, seg_ref, exp_ref, cnt_ref, gl_ref, place_ref,
                      hseg_ref, k_ref, vt_ref, n2_ref):
    c = c_ref[0].astype(BF16)
    kn = _dot(c, wk_ref[...])
    inv = _segment_inv_rms(kn, seg_ref, exp_ref, cnt_ref[...])
    k = kn * inv * gl_ref[...] + _dot(kr_ref[0].astype(BF16), place_ref[...])
    kb = k.astype(BF16)
    vt = _dot_nt(wvt_ref[...], c).astype(BF16)
    for h in range(MLA_HEADS):
        k_ref[0, h] = kb[:, h * HEAD_BLOCK:(h + 1) * HEAD_BLOCK]
        vt_ref[0, 0, h, :V_DIM, :] = vt[h * V_DIM:(h + 1) * V_DIM, :]
        vt_ref[0, 0, h, V_DIM:, :] = jnp.ones((BF16_SUBLANES, vt.shape[1]), BF16)
    _accumulate_max(n2_ref, _max_head_sq_norm(k, hseg_ref), pl.program_id(1) == 0)


def _kv_expand(ckv, kr, tm, w_k, w_vt, seg, exp, cnt, gain_lane, place, head_seg):
    batch, lp, _ = ckv.shape
    width = MLA_HEADS * HEAD_BLOCK
    cs = lambda shape: pl.BlockSpec(shape, lambda b, i: (0,) * len(shape))
    return pl.pallas_call(
        _kv_expand_kernel,
        grid=(batch, lp // tm),
        in_specs=[pl.BlockSpec((1, tm, KV_LORA), lambda b, i: (b, i, 0)),
                  pl.BlockSpec((1, tm, LANES), lambda b, i: (b, i, 0)),
                  cs((KV_LORA, width)), cs((MLA_HEADS * V_DIM, KV_LORA)),
                  cs((width, LANES)), cs((LANES, width)), cs((1, LANES)), cs((1, width)),
                  cs((LANES, width)), cs((width, LANES))],
        out_specs=[pl.BlockSpec((1, MLA_HEADS, tm, HEAD_BLOCK), lambda b, i: (b, 0, i, 0)),
                   pl.BlockSpec((1, 1, MLA_HEADS, V_AUG, tm), lambda b, i: (b, i, 0, 0, 0)),
                   pl.BlockSpec((1, 1, LANES), lambda b, i: (b, 0, 0))],
        out_shape=[jax.ShapeDtypeStruct((batch, MLA_HEADS, lp, HEAD_BLOCK), BF16),
                   jax.ShapeDtypeStruct((batch, lp // tm, MLA_HEADS, V_AUG, tm), BF16),
                   jax.ShapeDtypeStruct((batch, 1, LANES), F32)],
        compiler_params=_cparams(("arbitrary", "arbitrary")),
        name="kv_expand",
    )(ckv, kr, w_k, w_vt, seg, exp, cnt, gain_lane, place, head_seg)


def _mla_pre_kernel(x_ref, sh_ref, sc_ref, gain_ref, wdq_ref, qag_ref, wuq_ref, seg_ref, exp_ref,
                    cnt_ref, gl_ref, hseg_ref, cos_ref, sa_ref, sb_ref, q_ref, n2_ref, *, tiles_per_batch):
    h = _ada_norm(x_ref[...], gain_ref[...], sh_ref[0], sc_ref[0]).astype(BF16)
    qa = _dot(h, wdq_ref[...])
    qa = qa * lax.rsqrt(jnp.mean(qa * qa, axis=-1, keepdims=True) + EPS) * qag_ref[...]
    q = _dot(qa.astype(BF16), wuq_ref[...])
    qn = q * _segment_inv_rms(q, seg_ref, exp_ref, cnt_ref[...]) * gl_ref[...]
    cos, sa, sb = cos_ref[...], sa_ref[...], sb_ref[...]
    for hd in range(MLA_HEADS):
        blk = qn[:, hd * HEAD_BLOCK:(hd + 1) * HEAD_BLOCK]
        q_ref[0, hd] = _rope_block(blk, cos, sa, sb).astype(BF16)
    _accumulate_max(n2_ref, _max_head_sq_norm(qn, hseg_ref), pl.program_id(0) % tiles_per_batch == 0)


def _mla_pre(x2, mod, n_slots, slot0, tm, tpb, batch, seq, gain, w_dq, qa_gain, w_uq, seg, exp, cnt,
             gain_lane, head_seg, cos, sin_a, sin_b):
    n = x2.shape[0]
    width = MLA_HEADS * HEAD_BLOCK
    tab = pl.BlockSpec((tm, LANES), lambda i: (i % tpb, 0))
    return pl.pallas_call(
        functools.partial(_mla_pre_kernel, tiles_per_batch=tpb),
        grid=(n // tm,),
        in_specs=[pl.BlockSpec((tm, D_MODEL), lambda i: (i, 0)),
                  _mod_spec(slot0, n_slots, tpb), _mod_spec(slot0 + 1, n_slots, tpb),
                  _const_spec((1, D_MODEL)), _const_spec((D_MODEL, Q_LORA)), _const_spec((1, Q_LORA)),
                  _const_spec((Q_LORA, width)), _const_spec((width, LANES)), _const_spec((LANES, width)),
                  _const_spec((1, LANES)), _const_spec((1, width)), _const_spec((width, LANES)),
                  tab, tab, tab],
        out_specs=[pl.BlockSpec((1, MLA_HEADS, tm, HEAD_BLOCK), lambda i: (i // tpb, 0, i % tpb, 0)),
                   pl.BlockSpec((1, 1, LANES), lambda i: (i // tpb, 0, 0))],
        out_shape=[jax.ShapeDtypeStruct((batch, MLA_HEADS, seq, HEAD_BLOCK), BF16),
                   jax.ShapeDtypeStruct((batch, 1, LANES), F32)],
        compiler_params=_cparams(("arbitrary",)),
        name="mla_pre",
    )(x2, mod, mod, gain, w_dq, qa_gain, w_uq, seg, exp, cnt, gain_lane, head_seg, cos, sin_a, sin_b)


def _flash_kernel(qi_ref, ki_ref, qp_ref, kp_ref, fl_ref, bounded_ref, q_ref, k_ref, vt_ref, o_ref,
                  m_scr, acc_scr, pa_scr, pb_scr, aa_scr, ab_scr, *, tq, tk, q_offset, n_keys):
    p = pl.program_id(1)
    q_lo = q_offset + qi_ref[p] * tq
    k_lo = ki_ref[p] * tk
    first_a = ki_ref[p] == 0
    first_b = (fl_ref[p] & 1) == 1
    last_b = (fl_ref[p] & 2) == 2

    @pl.when(p == 0)
    def _():
        pb_scr[...] = jnp.zeros(pb_scr.shape, BF16)
        ab_scr[...] = jnp.ones(ab_scr.shape, F32)
        acc_scr[...] = jnp.zeros(acc_scr.shape, F32)
        m_scr[...] = jnp.full(m_scr.shape, -jnp.inf, F32)

    def step(masked, shifted, p_w, p_r, a_w, a_r):
        def scores(hd):
            s = _dot_nt(k_ref[0, hd], q_ref[0, hd])
            if masked:
                kidx = k_lo + lax.broadcasted_iota(jnp.int32, s.shape, 0)
                qpos = q_lo + lax.broadcasted_iota(jnp.int32, s.shape, 1)
                ok = ((jnp.right_shift(kidx, CHUNK_SHIFT) <= jnp.right_shift(qpos, CHUNK_SHIFT))
                      & (kidx < n_keys))
                s = jnp.where(ok, s, -jnp.inf)
            return s

        if shifted:
            def head(hd, carry):
                s = scores(hd)
                pv = _dot(vt_ref[0, 0, hd], p_r[hd])
                m_old = jnp.where(first_a, -jnp.inf, m_scr[hd])
                m_new = jnp.maximum(m_old, jnp.max(s, axis=0, keepdims=True))
                m_scr[hd] = m_new
                a_w[hd] = jnp.exp2(m_old - m_new)
                p_w[hd] = jnp.exp2(s - m_new).astype(BF16)
                acc_scr[hd] = jnp.where(first_b, pv, acc_scr[hd] * a_r[hd] + pv)
                return carry

            lax.fori_loop(0, MLA_HEADS, head, 0)
        else:
            for hd in range(MLA_HEADS):
                p_w[hd] = jnp.exp2(scores(hd)).astype(BF16)
            for hd in range(MLA_HEADS):
                pv = _dot(vt_ref[0, 0, hd], p_r[hd])
                acc_scr[hd] = jnp.where(first_b, pv, acc_scr[hd] + pv)

    full = ((jnp.right_shift(k_lo + tk - 1, CHUNK_SHIFT) <= jnp.right_shift(q_lo, CHUNK_SHIFT))
            & (k_lo + tk <= n_keys))
    bounded = bounded_ref[pl.program_id(0)] == 1
    even = p % 2 == 0
    for masked in (False, True):
        for shifted in (False, True):
            for write_a in (False, True):
                cond = ((jnp.logical_not(full) if masked else full)
                        & (jnp.logical_not(bounded) if shifted else bounded)
                        & (even if write_a else jnp.logical_not(even)))
                bufs = (pa_scr, pb_scr, aa_scr, ab_scr) if write_a else (pb_scr, pa_scr, ab_scr, aa_scr)
                pl.when(cond)(functools.partial(step, masked, shifted, *bufs))

    @pl.when(last_b)
    def _():
        def fin(hp, carry):
            halves = []
            for hh in range(2):
                a = acc_scr[2 * hp + hh]
                halves.append(a[:V_DIM] * (1.0 / a[V_DIM:V_DIM + 1]))
            o_ref[0, hp] = jnp.concatenate(halves, axis=0).T.astype(BF16)
            return carry
        lax.fori_loop(0, HEAD_PAIRS, fin, 0)


def _flash_schedule(n_q, tq, tk, q_offset, n_keys, n_kt):
    pairs = []
    for qi in range(n_q):
        q_hi_chunk = (q_offset + qi * tq + tq - 1) // CHUNK
        k_max = min(n_keys, (q_hi_chunk + 1) * CHUNK)
        kt = min(n_kt, -(-k_max // tk))
        for ki in range(kt):
            pairs.append((qi, ki, (1 if ki == 0 else 0) | (2 if ki == kt - 1 else 0)))
    stage_a = pairs + [pairs[-1]]
    stage_b = [(pairs[0][0], pairs[0][1], 1)] + pairs
    cols = [[t[0] for t in stage_a], [t[1] for t in stage_a],
            [t[0] for t in stage_b], [t[1] for t in stage_b], [t[2] for t in stage_b]]
    return [jnp.asarray(np.array(c, np.int32)) for c in cols]


def _flash(q, k, vt, bounded, tq, tk, q_offset, n_keys):
    batch, _, seq, _ = q.shape
    lp = k.shape[2]
    sched = _flash_schedule(seq // tq, tq, tk, q_offset, n_keys, lp // tk)
    n_steps = int(sched[0].shape[0])
    grid_spec = pltpu.PrefetchScalarGridSpec(
        num_scalar_prefetch=6,
        grid=(batch, n_steps),
        in_specs=[pl.BlockSpec((1, MLA_HEADS, tq, HEAD_BLOCK),
                               lambda b, p, qi, ki, qp, kp, fl, bd: (b, 0, qi[p], 0)),
                  pl.BlockSpec((1, MLA_HEADS, tk, HEAD_BLOCK),
                               lambda b, p, qi, ki, qp, kp, fl, bd: (b, 0, ki[p], 0)),
                  pl.BlockSpec((1, 1, MLA_HEADS, V_AUG, tk),
                               lambda b, p, qi, ki, qp, kp, fl, bd: (b, kp[p], 0, 0, 0))],
        out_specs=pl.BlockSpec((1, HEAD_PAIRS, tq, LANES),
                               lambda b, p, qi, ki, qp, kp, fl, bd: (b, 0, qp[p], 0)),
        scratch_shapes=[pltpu.VMEM((MLA_HEADS, 1, tq), F32), pltpu.VMEM((MLA_HEADS, V_AUG, tq), F32),
                        pltpu.VMEM((MLA_HEADS, tk, tq), BF16), pltpu.VMEM((MLA_HEADS, tk, tq), BF16),
                        pltpu.VMEM((MLA_HEADS, 1, tq), F32), pltpu.VMEM((MLA_HEADS, 1, tq), F32)],
    )
    return pl.pallas_call(
        functools.partial(_flash_kernel, tq=tq, tk=tk, q_offset=q_offset, n_keys=n_keys),
        grid_spec=grid_spec,
        out_shape=jax.ShapeDtypeStruct((batch, HEAD_PAIRS, seq, LANES), BF16),
        compiler_params=_cparams(("arbitrary", "arbitrary")),
        name="flash",
    )(*sched, bounded, q, k, vt)


def _head_block_tables():
    width = MLA_HEADS * HEAD_BLOCK
    seg_q = np.zeros((width, LANES), np.float32)
    seg_k = np.zeros((width, LANES), np.float32)
    place = np.zeros((LANES, width), np.float32)
    for h in range(MLA_HEADS):
        base = h * HEAD_BLOCK
        seg_q[base:base + NOPE_DIM, h] = 1.0
        seg_q[base + NOPE_DIM:base + NOPE_DIM + ROPE_DIM, MLA_HEADS + h] = 1.0
        seg_k[base:base + NOPE_DIM, h] = 1.0
        for d in range(ROPE_DIM):
            place[d, base + NOPE_DIM + d] = 1.0
    cnt_q = np.ones((1, LANES), np.float32)
    cnt_q[0, :MLA_HEADS] = 1.0 / NOPE_DIM
    cnt_q[0, MLA_HEADS:2 * MLA_HEADS] = 1.0 / ROPE_DIM
    cnt_k = np.ones((1, LANES), np.float32)
    cnt_k[0, :MLA_HEADS] = 1.0 / NOPE_DIM
    head_seg = np.zeros((width, LANES), np.float32)
    for h in range(MLA_HEADS):
        head_seg[h * HEAD_BLOCK:(h + 1) * HEAD_BLOCK, h] = 1.0
    return seg_q, seg_k, place, cnt_q, cnt_k, head_seg


def _to_head_blocks(w, per_head, parts):
    k_dim = w.shape[0]
    w3 = w.reshape(k_dim, MLA_HEADS, per_head)
    out = jnp.zeros((k_dim, MLA_HEADS, HEAD_BLOCK), w.dtype)
    for start, size, dest in parts:
        out = out.at[:, :, dest:dest + size].set(w3[:, :, start:start + size])
    return out.reshape(k_dim, MLA_HEADS * HEAD_BLOCK)


def _lane_gain(nope_gain, rope_gain, scale):
    blk = jnp.zeros((HEAD_BLOCK,), F32).at[:NOPE_DIM].set(nope_gain * scale)
    if rope_gain is not None:
        blk = blk.at[NOPE_DIM:NOPE_DIM + ROPE_DIM].set(rope_gain * scale)
    return jnp.tile(blk, MLA_HEADS).reshape(1, MLA_HEADS * HEAD_BLOCK)


def _rope_tables(positions, lane0):
    inv_freq = ROPE_BASE ** (-jnp.arange(ROPE_HALF, dtype=F32) / ROPE_HALF)
    ang = positions.astype(F32)[:, None] * inv_freq[None, :]
    cos, sin = jnp.cos(ang), jnp.sin(ang)
    t = positions.shape[0]
    lead_one = jnp.ones((t, lane0), F32)
    lead = jnp.zeros((t, lane0), F32)
    half = jnp.zeros((t, ROPE_HALF), F32)
    tail = jnp.zeros((t, LANES - lane0 - ROPE_DIM), F32)
    cos_t = jnp.concatenate([lead_one, cos, cos, tail], axis=1)
    sin_a = jnp.concatenate([lead, -sin, half, tail], axis=1)
    sin_b = jnp.concatenate([lead, half, sin, tail], axis=1)
    return cos_t, sin_a, sin_b


def _prep_weights(p):
    w = {}
    n_qkvr = 2 * GLA_DK + 2 * GLA_DV
    w["gla_main"] = p["gla_w_in"][:, :, :n_qkvr].astype(BF16)
    w["gla_a"] = jnp.pad(p["gla_w_in"][:, :, n_qkvr:], ((0, 0), (0, 0), (0, LANES - GLA_GATE_RANK))).astype(BF16)
    w["gla_a2"] = jnp.pad(p["gla_w_alpha2"], ((0, 0), (0, LANES - GLA_GATE_RANK), (0, 0))).astype(BF16)
    w["gla_ba"] = p["gla_b_alpha"].reshape(-1, 1, GLA_DK)
    w["gla_gain"] = p["gla_out_gain"].reshape(-1, 1, GLA_DV_H)
    w["gla_wo"] = p["gla_w_o"].astype(BF16)
    w["mla_wo"] = p["mla_w_o"].astype(BF16)
    w["wr_t"] = p["moe_w_router"].T.astype(BF16)
    w["rbias"] = p["moe_router_bias"].reshape(N_EXPERTS, 1)
    w["wgu"] = p["moe_w_gate_up"].astype(BF16)
    w["wd"] = p["moe_w_down"].astype(BF16)
    w["wsgu"] = p["moe_w_shared_gate_up"].astype(BF16)
    w["wsd"] = p["moe_w_shared_down"].astype(BF16)
    w["kv_wc"] = p["mla_w_dkv"][:, :KV_LORA].astype(BF16)
    w["kv_wr"] = jnp.pad(p["mla_w_dkv"][:, KV_LORA:], ((0, 0), (0, LANES - ROPE_DIM))).astype(BF16)
    w["ckv_gain"] = p["mla_ckv_gain"].reshape(1, KV_LORA)
    w["kr_gain"] = jnp.pad(p["mla_k_rope_gain"], (0, LANES - ROPE_DIM)).reshape(1, LANES)
    per_kv = NOPE_DIM + V_DIM
    w["w_uk"] = _to_head_blocks(p["mla_w_ukv"], per_kv, [(0, NOPE_DIM, 0)]).astype(BF16)
    w_uv = p["mla_w_ukv"].reshape(KV_LORA, MLA_HEADS, per_kv)[:, :, NOPE_DIM:]
    w["w_uv_t"] = w_uv.reshape(KV_LORA, MLA_HEADS * V_DIM).T.astype(BF16)
    per_q = NOPE_DIM + ROPE_DIM
    w["w_dq"] = p["mla_w_dq"].astype(BF16)
    w["w_uq"] = jnp.stack([
        _to_head_blocks(p["mla_w_uq"][j], per_q, [(0, per_q, 0)]) for j in range(p["mla_w_uq"].shape[0])
    ]).astype(BF16)
    scale = (NOPE_DIM + ROPE_DIM) ** -0.5 * LOG2_E
    w["q_lane_gain"] = jnp.stack([
        _lane_gain(p["mla_q_nope_gain"][j], p["mla_q_rope_gain"][j], scale)
        for j in range(p["mla_q_nope_gain"].shape[0])])
    w["k_lane_gain"] = _lane_gain(p["mla_k_nope_gain"], None, 1.0)
    seg_q, seg_k, place, cnt_q, cnt_k, head_seg = _head_block_tables()
    w["head_seg"] = jnp.asarray(head_seg, BF16)
    w["seg_q"] = jnp.asarray(seg_q, BF16)
    w["exp_q"] = jnp.asarray(seg_q.T, BF16)
    w["seg_k"] = jnp.asarray(seg_k, BF16)
    w["exp_k"] = jnp.asarray(seg_k.T, BF16)
    w["place"] = jnp.asarray(place, BF16)
    w["cnt_q"] = jnp.asarray(cnt_q)
    w["cnt_k"] = jnp.asarray(cnt_k)
    return w


def _round_up(a, b):
    return -(-a // b) * b


def _trunk(x, mod_layers, mod_kv, gla_state, cache_ckv, cache_krope, p, w):
    batch, seq, _ = x.shape
    depth = mod_layers.shape[0]
    n_a = p["gla_w_in"].shape[0]
    n = batch * seq
    tm = min(512, seq)
    tpb = seq // tm
    tm_moe = min(MOE_ROW_TILE, seq)
    x2 = x.reshape(n, D_MODEL)
    offset = 0 if cache_ckv is None else cache_ckv.shape[1]
    q_pos = offset + jnp.arange(seq, dtype=jnp.int32)
    kv_tabs = _rope_tables(q_pos, 0)
    q_tabs = _rope_tables(q_pos, NOPE_DIM)
    new_states = []
    ckv_new = krope_new = k_all = vt_all = k_n2 = None
    n_keys = 0
    tq = min(512, _round_up(seq, LANES))
    seq_q = _round_up(seq, tq)
    tk = 512
    for layer in range(depth):
        mod = mod_layers[layer].reshape(batch * 6, 1, D_MODEL)
        gain_m = p["norm_mix_gain"][layer].reshape(1, D_MODEL)
        gain_f = p["norm_ffn_gain"][layer].reshape(1, D_MODEL)
        if layer < n_a:
            q, k, v, r, la = _gla_pre(x2, mod, 6, 0, tm, tpb, gain_m, w["gla_main"][layer],
                                      w["gla_a"][layer], w["gla_a2"][layer], w["gla_ba"][layer])
            chunk = min(CHUNK, seq)
            rows_per_step = min(512, seq)
            s0_t = jnp.swapaxes(gla_state[layer], -1, -2)
            y, s_t = _gla_core(q, k, la, v, r, w["gla_gain"][layer], s0_t, batch, seq, chunk, rows_per_step)
            new_states.append(jnp.swapaxes(s_t, -1, -2))
            w_o = w["gla_wo"][layer]
        else:
            j = layer - n_a
            qh, q_n2 = _mla_pre(x2, mod, 6, 0, tm, tpb, batch, seq, gain_m, w["w_dq"][j],
                                p["mla_q_a_gain"][j].reshape(1, Q_LORA), w["w_uq"][j], w["seg_q"],
                                w["exp_q"], w["cnt_q"], w["q_lane_gain"][j], w["head_seg"], *q_tabs)
            if seq_q != seq:
                qh = jnp.pad(qh, ((0, 0), (0, 0), (0, seq_q - seq), (0, 0)))
            bounded = (jnp.max(q_n2 * k_n2, axis=(1, 2)) * NORM_SLACK <= LOGIT_LIMIT ** 2).astype(jnp.int32)
            y = _flash(qh, k_all, vt_all, bounded, tq, tk, offset, n_keys)
            if seq_q != seq:
                y = y[:, :, :seq, :]
            w_o = w["mla_wo"][j]
        if n <= MOE_ROW_TILE and batch > 1:
            post_mod, post_tm, post_tpb = jnp.repeat(mod_layers[layer], seq, axis=0), n, 0
        else:
            post_mod, post_tm, post_tpb = mod, tm_moe, seq // tm_moe
        x2 = _post(x2, y, post_mod, 6, 0, post_tm, post_tpb, w_o, gain_f, w["wr_t"], w["rbias"],
                   w["wgu"][layer], w["wd"][layer], w["wsgu"][layer], w["wsd"][layer])
        if layer == n_a - 1:
            mkv = mod_kv.reshape(batch * 2, 1, D_MODEL)
            ckv2, kr2 = _kv_latent(x2, mkv, tm, tpb, p["kv_norm_gain"].reshape(1, D_MODEL), w["kv_wc"],
                                   w["kv_wr"], w["ckv_gain"], w["kr_gain"], *kv_tabs)
            ckv_new = ckv2.reshape(batch, seq, KV_LORA)
            kr_pad = kr2.reshape(batch, seq, LANES)
            krope_new = kr_pad[:, :, :ROPE_DIM]
            if cache_ckv is None:
                ckv_all, kr_all = ckv_new, kr_pad
            else:
                ckv_all = jnp.concatenate([cache_ckv, ckv_new], axis=1)
                kr_all = jnp.concatenate(
                    [jnp.pad(cache_krope, ((0, 0), (0, 0), (0, LANES - ROPE_DIM))), kr_pad], axis=1)
            n_keys = ckv_all.shape[1]
            lp = _round_up(n_keys, tk)
            if lp != n_keys:
                ckv_all = jnp.pad(ckv_all, ((0, 0), (0, lp - n_keys), (0, 0)))
                kr_all = jnp.pad(kr_all, ((0, 0), (0, lp - n_keys), (0, 0)))
            k_all, vt_all, k_n2 = _kv_expand(ckv_all, kr_all, tk, w["w_uk"], w["w_uv_t"], w["seg_k"], w["exp_k"],
                                       w["cnt_k"], w["k_lane_gain"], w["place"], w["head_seg"])
    return x2.reshape(batch, seq, D_MODEL), jnp.stack(new_states), ckv_new, krope_new


def kernel(x_prompt, x_sample, c_prompt, c_sample, state_gla, cache_kv_latent, cache_k_rope, w_ada, b_ada, norm_mix_gain, norm_ffn_gain, gla_w_in, gla_w_alpha2, gla_b_alpha, gla_out_gain, gla_w_o, kv_w_ada, kv_b_ada, kv_norm_gain, mla_w_dkv, mla_ckv_gain, mla_k_rope_gain, mla_w_ukv, mla_k_nope_gain, mla_w_dq, mla_q_a_gain, mla_w_uq, mla_q_nope_gain, mla_q_rope_gain, mla_w_o, moe_w_router, moe_router_bias, moe_w_gate_up, moe_w_down, moe_w_shared_gate_up, moe_w_shared_down):
    p = dict(norm_mix_gain=norm_mix_gain, norm_ffn_gain=norm_ffn_gain, gla_w_in=gla_w_in,
             gla_w_alpha2=gla_w_alpha2, gla_b_alpha=gla_b_alpha, gla_out_gain=gla_out_gain,
             gla_w_o=gla_w_o, kv_norm_gain=kv_norm_gain, mla_w_dkv=mla_w_dkv, mla_ckv_gain=mla_ckv_gain,
             mla_k_rope_gain=mla_k_rope_gain, mla_w_ukv=mla_w_ukv, mla_k_nope_gain=mla_k_nope_gain,
             mla_w_dq=mla_w_dq, mla_q_a_gain=mla_q_a_gain, mla_w_uq=mla_w_uq,
             mla_q_nope_gain=mla_q_nope_gain, mla_q_rope_gain=mla_q_rope_gain, mla_w_o=mla_w_o,
             moe_w_router=moe_w_router, moe_router_bias=moe_router_bias, moe_w_gate_up=moe_w_gate_up,
             moe_w_down=moe_w_down, moe_w_shared_gate_up=moe_w_shared_gate_up,
             moe_w_shared_down=moe_w_shared_down)
    w = _prep_weights(p)
    depth = w_ada.shape[0]
    bp, bs = c_prompt.shape[0], c_sample.shape[0]
    rows = _round_up(bp + bs, 8)
    c_all = jnp.concatenate([c_prompt, c_sample, jnp.zeros((rows - bp - bs, D_MODEL), F32)], axis=0)
    mod_l = _modulation(c_all, w_ada, b_ada.reshape(depth, 1, 6 * D_MODEL))
    mod_k = _modulation(c_all, kv_w_ada.reshape(1, D_MODEL, 2 * D_MODEL),
                        kv_b_ada.reshape(1, 1, 2 * D_MODEL))[0]
    gla_zero = jnp.zeros((state_gla.shape[0], bp) + state_gla.shape[2:], x_prompt.dtype)
    y_p, s_p, ckv_p, kr_p = _trunk(x_prompt, mod_l[:, :bp], mod_k[:bp], gla_zero, None, None, p, w)
    y_s, s_s, ckv_s, kr_s = _trunk(x_sample, mod_l[:, bp:bp + bs], mod_k[bp:bp + bs], state_gla,
                                   cache_kv_latent, cache_k_rope, p, w)
    return (y_p, y_s, s_p, ckv_p, kr_p, s_s, ckv_s, kr_s)
```

```python
import functools
import math

import numpy as np
import jax
import jax.numpy as jnp
from jax import lax
from jax.experimental import pallas as pl
from jax.experimental.pallas import tpu as pltpu

F32 = jnp.float32
BF16 = jnp.bfloat16

D_MODEL = 1024
CHUNK = 64
CHUNK_SHIFT = 6
GLA_HEADS = 4
GLA_DK = 512
GLA_DV = 1024
GLA_DK_H = 128
GLA_DV_H = 256
GLA_GATE_RANK = 16
GLA_GATE_TAU = 16.0
MLA_HEADS = 16
Q_LORA = 256
KV_LORA = 128
NOPE_DIM = 64
ROPE_DIM = 32
ROPE_HALF = 16
V_DIM = 64
ROPE_BASE = 10000.0
N_EXPERTS = 16
N_GROUPS = 4
EXPERTS_PER_GROUP = 4
EXPERT_FF = 256
SHARED_FF = 256
EPS = 1e-6

LANES = 128
HEAD_BLOCK = LANES
HEAD_PAIRS = MLA_HEADS // 2
BF16_SUBLANES = 16
V_AUG = V_DIM + BF16_SUBLANES
FLASH_PV_LAG = 1
EXPERTS_PER_STEP = 4
MOE_ROW_TILE = 1024
LOG2_E = math.log2(math.e)
LOGIT_LIMIT = 40.0
NORM_SLACK = 1.05
VMEM_LIMIT = 56 * 1024 * 1024


def _cparams(sem):
    return pltpu.CompilerParams(dimension_semantics=sem, vmem_limit_bytes=VMEM_LIMIT)


def _dot(a, b):
    return jnp.dot(a, b, preferred_element_type=F32)


def _dot_nt(a, b):
    return lax.dot_general(a, b, (((1,), (1,)), ((), ())), preferred_element_type=F32)


def _dot_tn(a, b):
    return lax.dot_general(a, b, (((0,), (0,)), ((), ())), preferred_element_type=F32)


def _silu(x):
    return x * jax.nn.sigmoid(x)


def _split2(x):
    hi = x.astype(BF16)
    lo = (x - hi.astype(F32)).astype(BF16)
    return hi, lo


def _ada_norm(x, gain, shift, scale):
    y = x * lax.rsqrt(jnp.mean(x * x, axis=-1, keepdims=True) + EPS)
    return (y * gain) * (1.0 + scale) + shift


def _mod_value(ref):
    return ref[0] if len(ref.shape) == 3 else ref[...]


def _const_spec(shape):
    nd = len(shape)
    return pl.BlockSpec(shape, lambda *_: (0,) * nd)


def _mod_spec(slot, n_slots, tiles_per_batch):
    return pl.BlockSpec((1, 1, D_MODEL),
                        lambda i, *_: ((i // tiles_per_batch) * n_slots + slot, 0, 0))


def _mod_kernel(c_ref, w_ref, b_ref, o_ref):
    c = c_ref[...]
    o_ref[0] = _dot(_silu(c).astype(BF16), w_ref[0].astype(BF16)) + b_ref[0]


def _modulation(c_pad, w, b):
    n_layers, _, n_out = w.shape
    rows = c_pad.shape[0]
    tn = 1024
    return pl.pallas_call(
        _mod_kernel,
        grid=(n_layers, n_out // tn),
        in_specs=[pl.BlockSpec((rows, D_MODEL), lambda l, j: (0, 0)),
                  pl.BlockSpec((1, D_MODEL, tn), lambda l, j: (l, 0, j)),
                  pl.BlockSpec((1, 1, tn), lambda l, j: (l, 0, j))],
        out_specs=pl.BlockSpec((1, rows, tn), lambda l, j: (l, 0, j)),
        out_shape=jax.ShapeDtypeStruct((n_layers, rows, n_out), F32),
        compiler_params=_cparams(("arbitrary", "arbitrary")),
        name="modulation",
    )(c_pad, w, b)


def _log_sigmoid(z):
    return jnp.minimum(z, 0.0) - jnp.log1p(jnp.exp(-jnp.abs(z)))


def _gla_pre_kernel(x_ref, sh_ref, sc_ref, gain_ref, w_ref, wa_ref, wa2_ref, ba_ref,
                    q_ref, k_ref, v_ref, r_ref, la_ref):
    h = _ada_norm(x_ref[...], gain_ref[...], sh_ref[0], sc_ref[0]).astype(BF16)
    y = _dot(h, w_ref[...])
    q_ref[...] = y[:, :GLA_DK] * (GLA_DK_H ** -0.5)
    k_ref[...] = y[:, GLA_DK:2 * GLA_DK]
    v_ref[...] = y[:, 2 * GLA_DK:2 * GLA_DK + GLA_DV].astype(BF16)
    r_ref[...] = y[:, 2 * GLA_DK + GLA_DV:]
    a = _dot(h, wa_ref[...])
    z = _dot(a.astype(BF16), wa2_ref[...]) + ba_ref[...]
    la_ref[...] = _log_sigmoid(z) * (1.0 / GLA_GATE_TAU)


def _gla_pre(x2, mod, n_slots, slot0, tm, tpb, gain, w_main, w_a, w_a2, b_a):
    n = x2.shape[0]
    row = lambda width: pl.BlockSpec((tm, width), lambda i: (i, 0))
    return pl.pallas_call(
        _gla_pre_kernel,
        grid=(n // tm,),
        in_specs=[row(D_MODEL), _mod_spec(slot0, n_slots, tpb), _mod_spec(slot0 + 1, n_slots, tpb),
                  _const_spec((1, D_MODEL)), _const_spec(w_main.shape), _const_spec(w_a.shape),
                  _const_spec(w_a2.shape), _const_spec((1, GLA_DK))],
        out_specs=[row(GLA_DK), row(GLA_DK), row(GLA_DV), row(GLA_DV), row(GLA_DK)],
        out_shape=[jax.ShapeDtypeStruct((n, GLA_DK), F32), jax.ShapeDtypeStruct((n, GLA_DK), F32),
                   jax.ShapeDtypeStruct((n, GLA_DV), BF16), jax.ShapeDtypeStruct((n, GLA_DV), F32),
                   jax.ShapeDtypeStruct((n, GLA_DK), F32)],
        compiler_params=_cparams(("arbitrary",)),
        name="gla_pre",
    )(x2, mod, mod, gain, w_main, w_a, w_a2, b_a)


def _gla_core_kernel(q_ref, k_ref, la_ref, v_ref, r_ref, gain_ref, s0_ref,
                     o_ref, st_ref, s_scr, *, chunk, n_chunks):
    j = pl.program_id(0)
    batch = q_ref.shape[0]

    @pl.when(j == 0)
    def _():
        s_scr[...] = s0_ref[...]

    ri = lax.broadcasted_iota(jnp.int32, (chunk, chunk), 0)
    ci = lax.broadcasted_iota(jnp.int32, (chunk, chunk), 1)
    causal = ri >= ci
    tri = causal.astype(BF16)
    gain = gain_ref[...]

    def body(c, carry):
        rows = pl.ds(pl.multiple_of(c * chunk, chunk), chunk)
        for bi in range(batch):
            g = la_ref[bi, rows, :]
            g_hi = g.astype(BF16)
            g_r1 = g - g_hi.astype(F32)
            g_mid = g_r1.astype(BF16)
            g_lo = (g_r1 - g_mid.astype(F32)).astype(BF16)
            b = _dot(tri, g_hi) + _dot(tri, g_mid) + _dot(tri, g_lo)
            b_last = b[chunk - 1:chunk, :]
            q = q_ref[bi, rows, :]
            k = k_ref[bi, rows, :]
            q_dec = (q * jnp.exp(b)).astype(BF16)
            k_inv = (k * jnp.exp(-b)).astype(BF16)
            k_end = (k * jnp.exp(b_last - b)).astype(BF16)
            dec = jnp.exp(b_last)
            for h in range(GLA_HEADS):
                ks = slice(h * GLA_DK_H, (h + 1) * GLA_DK_H)
                vs = slice(h * GLA_DV_H, (h + 1) * GLA_DV_H)
                v_h = v_ref[bi, rows, vs]
                s_t = s_scr[bi, h]
                sc = jnp.where(causal, _dot_nt(q_dec[:, ks], k_inv[:, ks]), 0.0)
                o = _dot(sc.astype(BF16), v_h) + _dot_nt(q_dec[:, ks], s_t.astype(BF16))
                s_scr[bi, h] = s_t * dec[:, ks] + _dot_tn(v_h, k_end[:, ks])
                on = o * lax.rsqrt(jnp.mean(o * o, axis=-1, keepdims=True) + EPS) * gain
                out = (on * _silu(r_ref[bi, rows, vs])).astype(BF16)
                o_ref[bi, 2 * h, rows, :] = out[:, :LANES]
                o_ref[bi, 2 * h + 1, rows, :] = out[:, LANES:]
        return carry

    lax.fori_loop(0, n_chunks, body, 0)

    @pl.when(j == pl.num_programs(0) - 1)
    def _():
        st_ref[...] = s_scr[...]


def _gla_core(q, k, la, v, r, gain, s0_t, batch, seq, chunk, rows_per_step):
    n_chunks = rows_per_step // chunk
    row = lambda width: pl.BlockSpec((batch, rows_per_step, width), lambda j: (0, j, 0))
    state = pl.BlockSpec((batch, GLA_HEADS, GLA_DV_H, GLA_DK_H), lambda j: (0, 0, 0, 0))
    as3d = lambda a: a.reshape(batch, seq, a.shape[-1])
    return pl.pallas_call(
        functools.partial(_gla_core_kernel, chunk=chunk, n_chunks=n_chunks),
        grid=(seq // rows_per_step,),
        in_specs=[row(GLA_DK), row(GLA_DK), row(GLA_DK), row(GLA_DV), row(GLA_DV),
                  _const_spec((1, GLA_DV_H)), state],
        out_specs=[pl.BlockSpec((batch, HEAD_PAIRS, rows_per_step, LANES), lambda j: (0, 0, j, 0)),
                   state],
        out_shape=[jax.ShapeDtypeStruct((batch, HEAD_PAIRS, seq, LANES), BF16),
                   jax.ShapeDtypeStruct((batch, GLA_HEADS, GLA_DV_H, GLA_DK_H), F32)],
        scratch_shapes=[pltpu.VMEM((batch, GLA_HEADS, GLA_DV_H, GLA_DK_H), F32)],
        compiler_params=_cparams(("arbitrary",)),
        name="gla_core",
    )(as3d(q), as3d(k), as3d(la), as3d(v), as3d(r), gain, s0_t)


def _top2_sum(a, b, c, d):
    hi1, lo1 = jnp.maximum(a, b), jnp.minimum(a, b)
    hi2, lo2 = jnp.maximum(c, d), jnp.minimum(c, d)
    return jnp.maximum(hi1, hi2) + jnp.maximum(jnp.minimum(hi1, hi2), jnp.maximum(lo1, lo2))


def _first_argmax(vals):
    idx = jnp.zeros(vals[0].shape, jnp.int32)
    best = vals[0]
    for j in range(1, len(vals)):
        upd = vals[j] > best
        idx = jnp.where(upd, j, idx)
        best = jnp.where(upd, vals[j], best)
    return idx


def _pick(idx, vals):
    out = vals[-1]
    for j in range(len(vals) - 2, -1, -1):
        out = jnp.where(idx == j, vals[j], out)
    return out


def _router_gates_t(logits_t, bias_col):
    s_t = jax.nn.sigmoid(logits_t)
    b_t = s_t + bias_col
    s = [s_t[e:e + 1, :] for e in range(N_EXPERTS)]
    b = [b_t[e:e + 1, :] for e in range(N_EXPERTS)]
    group_score = [_top2_sum(*b[4 * g:4 * g + 4]) for g in range(N_GROUPS)]
    best = _first_argmax(group_score)
    vb = [_pick(best, [b[4 * g + j] for g in range(N_GROUPS)]) for j in range(EXPERTS_PER_GROUP)]
    vs = [_pick(best, [s[4 * g + j] for g in range(N_GROUPS)]) for j in range(EXPERTS_PER_GROUP)]
    i1 = _first_argmax(vb)
    i2 = _first_argmax([jnp.where(i1 == j, -jnp.inf, vb[j]) for j in range(EXPERTS_PER_GROUP)])
    sel1 = _pick(i1, vs)
    sel2 = _pick(i2, vs)
    den = sel1 + sel2
    w1 = sel1 / den
    w2 = sel2 / den
    gates = []
    for g in range(N_GROUPS):
        for j in range(EXPERTS_PER_GROUP):
            val = jnp.where(i1 == j, w1, 0.0) + jnp.where(i2 == j, w2, 0.0)
            gates.append(jnp.where(best == g, val, 0.0))
    return gates


def _post_kernel(x_ref, y_ref, wo_ref, gm_ref, shf_ref, scf_ref, gf_ref, gain_ref,
                 wr_ref, rb_ref, wgu_ref, wd_ref, wsgu_ref, wsd_ref,
                 o_ref, xn_scr, h_scr, gate_scr, acc_scr):
    e = pl.program_id(1)
    tm = x_ref.shape[0]

    @pl.when(e == 0)
    def _():
        y = jnp.concatenate([y_ref[:, p].reshape(tm, LANES) for p in range(HEAD_PAIRS)], axis=-1)
        xn = x_ref[...] + _mod_value(gm_ref) * _dot(y, wo_ref[...])
        xn_scr[...] = xn
        h = _ada_norm(xn, gain_ref[...], _mod_value(shf_ref), _mod_value(scf_ref)).astype(BF16)
        h_scr[...] = h
        gates = _router_gates_t(_dot_nt(wr_ref[...], h), rb_ref[...])
        gate_t = jnp.concatenate(gates + [jnp.zeros((LANES - N_EXPERTS, tm), F32)], axis=0)
        gate_scr[...] = gate_t.T
        sgu = _dot(h, wsgu_ref[...])
        act = _silu(sgu[:, :SHARED_FF]) * sgu[:, SHARED_FF:]
        acc_scr[...] = _dot(act.astype(BF16), wsd_ref[...])

    lane = lax.broadcasted_iota(jnp.int32, (tm, LANES), 1)
    gate = gate_scr[...]
    h = h_scr[...]
    acts = []
    for j in range(EXPERTS_PER_STEP):
        gu = _dot(h, wgu_ref[j])
        gcol = jnp.sum(jnp.where(lane == e * EXPERTS_PER_STEP + j, gate, 0.0), axis=-1, keepdims=True)
        acts.append((_silu(gu[:, :EXPERT_FF]) * gu[:, EXPERT_FF:] * gcol).astype(BF16))
    w_down = wd_ref[...].reshape(EXPERTS_PER_STEP * EXPERT_FF, D_MODEL)
    acc_scr[...] += _dot(jnp.concatenate(acts, axis=-1), w_down)

    @pl.when(e == N_EXPERTS // EXPERTS_PER_STEP - 1)
    def _():
        o_ref[...] = xn_scr[...] + _mod_value(gf_ref) * acc_scr[...]


def _post(x2, y, mod, n_slots, slot0, tm, tpb, w_o, gain_f, wr_t, rbias, wgu, wd, wsgu, wsd):
    n = x2.shape[0]
    batch, _, seq, _ = y.shape
    if tpb:
        ms = lambda s: pl.BlockSpec((1, 1, D_MODEL),
                                    lambda i, e: ((i // tpb) * n_slots + slot0 + s, 0, 0))
        y_spec = pl.BlockSpec((1, HEAD_PAIRS, tm, LANES), lambda i, e: (i // tpb, 0, i % tpb, 0))
    else:
        ms = lambda s: pl.BlockSpec((tm, D_MODEL), lambda i, e: (i, slot0 + s))
        y_spec = pl.BlockSpec((tm // seq, HEAD_PAIRS, seq, LANES), lambda i, e: (i, 0, 0, 0))
    cs = lambda shape: pl.BlockSpec(shape, lambda i, e: (0,) * len(shape))
    return pl.pallas_call(
        _post_kernel,
        grid=(n // tm, N_EXPERTS // EXPERTS_PER_STEP),
        in_specs=[pl.BlockSpec((tm, D_MODEL), lambda i, e: (i, 0)),
                  y_spec,
                  cs((D_MODEL, D_MODEL)),
                  ms(2), ms(3), ms(4), ms(5), cs((1, D_MODEL)),
                  cs((N_EXPERTS, D_MODEL)), cs((N_EXPERTS, 1)),
                  pl.BlockSpec((EXPERTS_PER_STEP, D_MODEL, 2 * EXPERT_FF), lambda i, e: (e, 0, 0)),
                  pl.BlockSpec((EXPERTS_PER_STEP, EXPERT_FF, D_MODEL), lambda i, e: (e, 0, 0)),
                  cs((D_MODEL, 2 * SHARED_FF)), cs((SHARED_FF, D_MODEL))],
        out_specs=pl.BlockSpec((tm, D_MODEL), lambda i, e: (i, 0)),
        out_shape=jax.ShapeDtypeStruct((n, D_MODEL), F32),
        scratch_shapes=[pltpu.VMEM((tm, D_MODEL), F32), pltpu.VMEM((tm, D_MODEL), BF16),
                        pltpu.VMEM((tm, LANES), F32), pltpu.VMEM((tm, D_MODEL), F32)],
        compiler_params=_cparams(("arbitrary", "arbitrary")),
        name="post_moe",
    )(x2, y, w_o, mod, mod, mod, mod, gain_f, wr_t, rbias, wgu, wd, wsgu, wsd)


def _rope_block(x, cos, sin_a, sin_b):
    return (x * cos + pltpu.roll(x, LANES - ROPE_HALF, 1) * sin_a
            + pltpu.roll(x, ROPE_HALF, 1) * sin_b)


def _kv_latent_kernel(x_ref, sh_ref, sc_ref, gain_ref, wc_ref, wr_ref, cg_ref, rg_ref,
                      cos_ref, sa_ref, sb_ref, ckv_ref, kr_ref):
    h = _ada_norm(x_ref[...], gain_ref[...], sh_ref[0], sc_ref[0]).astype(BF16)
    c_raw = _dot(h, wc_ref[...])
    ckv_ref[...] = c_raw * lax.rsqrt(jnp.mean(c_raw * c_raw, axis=-1, keepdims=True) + EPS) * cg_ref[...]
    r_raw = _dot(h, wr_ref[...])
    ms = jnp.sum(r_raw * r_raw, axis=-1, keepdims=True) * (1.0 / ROPE_DIM)
    rn = r_raw * lax.rsqrt(ms + EPS) * rg_ref[...]
    kr_ref[...] = _rope_block(rn, cos_ref[...], sa_ref[...], sb_ref[...])


def _kv_latent(x2, mod, tm, tpb, gain, w_c, w_r, c_gain, r_gain, cos, sin_a, sin_b):
    n = x2.shape[0]
    row = lambda width: pl.BlockSpec((tm, width), lambda i: (i, 0))
    tab = pl.BlockSpec((tm, LANES), lambda i: (i % tpb, 0))
    return pl.pallas_call(
        _kv_latent_kernel,
        grid=(n // tm,),
        in_specs=[row(D_MODEL), _mod_spec(0, 2, tpb), _mod_spec(1, 2, tpb), _const_spec((1, D_MODEL)),
                  _const_spec((D_MODEL, KV_LORA)), _const_spec((D_MODEL, LANES)),
                  _const_spec((1, KV_LORA)), _const_spec((1, LANES)), tab, tab, tab],
        out_specs=[row(KV_LORA), row(LANES)],
        out_shape=[jax.ShapeDtypeStruct((n, KV_LORA), F32), jax.ShapeDtypeStruct((n, LANES), F32)],
        compiler_params=_cparams(("arbitrary",)),
        name="kv_latent",
    )(x2, mod, mod, gain, w_c, w_r, c_gain, r_gain, cos, sin_a, sin_b)


def _segment_inv_rms(x, seg_ref, exp_ref, inv_count):
    ss = _dot((x * x).astype(BF16), seg_ref[...])
    inv = lax.rsqrt(ss * inv_count + EPS)
    ihi, ilo = _split2(inv)
    return _dot(ihi, exp_ref[...]) + _dot(ilo, exp_ref[...])


def _max_head_sq_norm(x, head_seg_ref):
    n2 = _dot((x * x).astype(BF16), head_seg_ref[...])
    return jnp.max(n2, axis=0, keepdims=True)


def _accumulate_max(ref, value, first):
    @pl.when(first)
    def _():
        ref[0] = value

    @pl.when(jnp.logical_not(first))
    def _():
        ref[0] = jnp.maximum(ref[0], value)


def _kv_expand_kernel(c_ref, kr_ref, wk_ref, wvt_ref, seg_ref, exp_ref, cnt_ref, gl_ref, place_ref,
                      hseg_ref, k_ref, vt_ref, n2_ref):
    c = c_ref[0].astype(BF16)
    kn = _dot(c, wk_ref[...])
    inv = _segment_inv_rms(kn, seg_ref, exp_ref, cnt_ref[...])
    k = kn * inv * gl_ref[...] + _dot(kr_ref[0].astype(BF16), place_ref[...])
    kb = k.astype(BF16)
    vt = _dot_nt(wvt_ref[...], c).astype(BF16)
    for h in range(MLA_HEADS):
        k_ref[0, h] = kb[:, h * HEAD_BLOCK:(h + 1) * HEAD_BLOCK]
        vt_ref[0, 0, h, :V_DIM, :] = vt[h * V_DIM:(h + 1) * V_DIM, :]
        vt_ref[0, 0, h, V_DIM:, :] = jnp.ones((BF16_SUBLANES, vt.shape[1]), BF16)
    _accumulate_max(n2_ref, _max_head_sq_norm(k, hseg_ref), pl.program_id(1) == 0)


def _kv_expand(ckv, kr, tm, w_k, w_vt, seg, exp, cnt, gain_lane, place, head_seg):
    batch, lp, _ = ckv.shape
    width = MLA_HEADS * HEAD_BLOCK
    cs = lambda shape: pl.BlockSpec(shape, lambda b, i: (0,) * len(shape))
    return pl.pallas_call(
        _kv_expand_kernel,
        grid=(batch, lp // tm),
        in_specs=[pl.BlockSpec((1, tm, KV_LORA), lambda b, i: (b, i, 0)),
                  pl.BlockSpec((1, tm, LANES), lambda b, i: (b, i, 0)),
                  cs((KV_LORA, width)), cs((MLA_HEADS * V_DIM, KV_LORA)),
                  cs((width, LANES)), cs((LANES, width)), cs((1, LANES)), cs((1, width)),
                  cs((LANES, width)), cs((width, LANES))],
        out_specs=[pl.BlockSpec((1, MLA_HEADS, tm, HEAD_BLOCK), lambda b, i: (b, 0, i, 0)),
                   pl.BlockSpec((1, 1, MLA_HEADS, V_AUG, tm), lambda b, i: (b, i, 0, 0, 0)),
                   pl.BlockSpec((1, 1, LANES), lambda b, i: (b, 0, 0))],
        out_shape=[jax.ShapeDtypeStruct((batch, MLA_HEADS, lp, HEAD_BLOCK), BF16),
                   jax.ShapeDtypeStruct((batch, lp // tm, MLA_HEADS, V_AUG, tm), BF16),
                   jax.ShapeDtypeStruct((batch, 1, LANES), F32)],
        compiler_params=_cparams(("arbitrary", "arbitrary")),
        name="kv_expand",
    )(ckv, kr, w_k, w_vt, seg, exp, cnt, gain_lane, place, head_seg)


def _mla_pre_kernel(x_ref, sh_ref, sc_ref, gain_ref, wdq_ref, qag_ref, wuq_ref, seg_ref, exp_ref,
                    cnt_ref, gl_ref, hseg_ref, cos_ref, sa_ref, sb_ref, q_ref, n2_ref, *, tiles_per_batch):
    h = _ada_norm(x_ref[...], gain_ref[...], sh_ref[0], sc_ref[0]).astype(BF16)
    qa = _dot(h, wdq_ref[...])
    qa = qa * lax.rsqrt(jnp.mean(qa * qa, axis=-1, keepdims=True) + EPS) * qag_ref[...]
    q = _dot(qa.astype(BF16), wuq_ref[...])
    qn = q * _segment_inv_rms(q, seg_ref, exp_ref, cnt_ref[...]) * gl_ref[...]
    cos, sa, sb = cos_ref[...], sa_ref[...], sb_ref[...]
    for hd in range(MLA_HEADS):
        blk = qn[:, hd * HEAD_BLOCK:(hd + 1) * HEAD_BLOCK]
        q_ref[0, hd] = _rope_block(blk, cos, sa, sb).T.astype(BF16)
    _accumulate_max(n2_ref, _max_head_sq_norm(qn, hseg_ref), pl.program_id(0) % tiles_per_batch == 0)


def _mla_pre(x2, mod, n_slots, slot0, tm, tpb, batch, seq, gain, w_dq, qa_gain, w_uq, seg, exp, cnt,
             gain_lane, head_seg, cos, sin_a, sin_b):
    n = x2.shape[0]
    width = MLA_HEADS * HEAD_BLOCK
    tab = pl.BlockSpec((tm, LANES), lambda i: (i % tpb, 0))
    return pl.pallas_call(
        functools.partial(_mla_pre_kernel, tiles_per_batch=tpb),
        grid=(n // tm,),
        in_specs=[pl.BlockSpec((tm, D_MODEL), lambda i: (i, 0)),
                  _mod_spec(slot0, n_slots, tpb), _mod_spec(slot0 + 1, n_slots, tpb),
                  _const_spec((1, D_MODEL)), _const_spec((D_MODEL, Q_LORA)), _const_spec((1, Q_LORA)),
                  _const_spec((Q_LORA, width)), _const_spec((width, LANES)), _const_spec((LANES, width)),
                  _const_spec((1, LANES)), _const_spec((1, width)), _const_spec((width, LANES)),
                  tab, tab, tab],
        out_specs=[pl.BlockSpec((1, MLA_HEADS, HEAD_BLOCK, tm), lambda i: (i // tpb, 0, 0, i % tpb)),
                   pl.BlockSpec((1, 1, LANES), lambda i: (i // tpb, 0, 0))],
        out_shape=[jax.ShapeDtypeStruct((batch, MLA_HEADS, HEAD_BLOCK, seq), BF16),
                   jax.ShapeDtypeStruct((batch, 1, LANES), F32)],
        compiler_params=_cparams(("arbitrary",)),
        name="mla_pre",
    )(x2, mod, mod, gain, w_dq, qa_gain, w_uq, seg, exp, cnt, gain_lane, head_seg, cos, sin_a, sin_b)


def _flash_kernel(qi_ref, ki_ref, qp_ref, kp_ref, fl_ref, bounded_ref, q_ref, k_ref, vt_ref, o_ref,
                  m_scr, acc_scr, pa_scr, pb_scr, aa_scr, ab_scr, *, tq, tk, q_offset, n_keys):
    p = pl.program_id(1)
    q_lo = q_offset + qi_ref[p] * tq
    k_lo = ki_ref[p] * tk
    first_a = ki_ref[p] == 0
    first_b = (fl_ref[p] & 1) == 1
    last_b = (fl_ref[p] & 2) == 2

    @pl.when(p == 0)
    def _():
        pb_scr[...] = jnp.zeros(pb_scr.shape, BF16)
        ab_scr[...] = jnp.ones(ab_scr.shape, F32)
        acc_scr[...] = jnp.zeros(acc_scr.shape, F32)
        m_scr[...] = jnp.full(m_scr.shape, -jnp.inf, F32)

    def step(masked, shifted, p_w, p_r, a_w, a_r):
        def scores(hd):
            s = _dot(k_ref[0, hd], q_ref[0, hd])
            if masked:
                kidx = k_lo + lax.broadcasted_iota(jnp.int32, s.shape, 0)
                qpos = q_lo + lax.broadcasted_iota(jnp.int32, s.shape, 1)
                ok = ((jnp.right_shift(kidx, CHUNK_SHIFT) <= jnp.right_shift(qpos, CHUNK_SHIFT))
                      & (kidx < n_keys))
                s = jnp.where(ok, s, -jnp.inf)
            return s

        if shifted:
            def head(hd, carry):
                s = scores(hd)
                pv = _dot(vt_ref[0, 0, hd], p_r[hd])
                m_old = jnp.where(first_a, -jnp.inf, m_scr[hd])
                m_new = jnp.maximum(m_old, jnp.max(s, axis=0, keepdims=True))
                m_scr[hd] = m_new
                a_w[hd] = jnp.exp2(m_old - m_new)
                p_w[hd] = jnp.exp2(s - m_new).astype(BF16)
                acc_scr[hd] = jnp.where(first_b, pv, acc_scr[hd] * a_r[hd] + pv)
                return carry

            lax.fori_loop(0, MLA_HEADS, head, 0)
        else:
            for step_i in range(MLA_HEADS + FLASH_PV_LAG):
                if step_i < MLA_HEADS:
                    p_w[step_i] = jnp.exp2(scores(step_i)).astype(BF16)
                hd = step_i - FLASH_PV_LAG
                if hd >= 0:
                    pv = _dot(vt_ref[0, 0, hd], p_r[hd])
                    acc_scr[hd] = jnp.where(first_b, pv, acc_scr[hd] + pv)

    full = ((jnp.right_shift(k_lo + tk - 1, CHUNK_SHIFT) <= jnp.right_shift(q_lo, CHUNK_SHIFT))
            & (k_lo + tk <= n_keys))
    bounded = bounded_ref[pl.program_id(0)] == 1
    even = p % 2 == 0
    for masked in (False, True):
        for shifted in (False, True):
            for write_a in (False, True):
                cond = ((jnp.logical_not(full) if masked else full)
                        & (jnp.logical_not(bounded) if shifted else bounded)
                        & (even if write_a else jnp.logical_not(even)))
                bufs = (pa_scr, pb_scr, aa_scr, ab_scr) if write_a else (pb_scr, pa_scr, ab_scr, aa_scr)
                pl.when(cond)(functools.partial(step, masked, shifted, *bufs))

    @pl.when(last_b)
    def _():
        def fin(hp, carry):
            halves = []
            for hh in range(2):
                a = acc_scr[2 * hp + hh]
                halves.append(a[:V_DIM] * (1.0 / a[V_DIM:V_DIM + 1]))
            o_ref[0, hp] = jnp.concatenate(halves, axis=0).T.astype(BF16)
            return carry
        lax.fori_loop(0, HEAD_PAIRS, fin, 0)


def _flash_schedule(n_q, tq, tk, q_offset, n_keys, n_kt):
    pairs = []
    for qi in range(n_q):
        q_hi_chunk = (q_offset + qi * tq + tq - 1) // CHUNK
        k_max = min(n_keys, (q_hi_chunk + 1) * CHUNK)
        kt = min(n_kt, -(-k_max // tk))
        for ki in range(kt):
            pairs.append((qi, ki, (1 if ki == 0 else 0) | (2 if ki == kt - 1 else 0)))
    stage_a = pairs + [pairs[-1]]
    stage_b = [(pairs[0][0], pairs[0][1], 1)] + pairs
    cols = [[t[0] for t in stage_a], [t[1] for t in stage_a],
            [t[0] for t in stage_b], [t[1] for t in stage_b], [t[2] for t in stage_b]]
    return [jnp.asarray(np.array(c, np.int32)) for c in cols]


def _flash(q, k, vt, bounded, tq, tk, q_offset, n_keys):
    batch, _, _, seq = q.shape
    lp = k.shape[2]
    sched = _flash_schedule(seq // tq, tq, tk, q_offset, n_keys, lp // tk)
    n_steps = int(sched[0].shape[0])
    grid_spec = pltpu.PrefetchScalarGridSpec(
        num_scalar_prefetch=6,
        grid=(batch, n_steps),
        in_specs=[pl.BlockSpec((1, MLA_HEADS, HEAD_BLOCK, tq),
                               lambda b, p, qi, ki, qp, kp, fl, bd: (b, 0, 0, qi[p])),
                  pl.BlockSpec((1, MLA_HEADS, tk, HEAD_BLOCK),
                               lambda b, p, qi, ki, qp, kp, fl, bd: (b, 0, ki[p], 0)),
                  pl.BlockSpec((1, 1, MLA_HEADS, V_AUG, tk),
                               lambda b, p, qi, ki, qp, kp, fl, bd: (b, kp[p], 0, 0, 0))],
        out_specs=pl.BlockSpec((1, HEAD_PAIRS, tq, LANES),
                               lambda b, p, qi, ki, qp, kp, fl, bd: (b, 0, qp[p], 0)),
        scratch_shapes=[pltpu.VMEM((MLA_HEADS, 1, tq), F32), pltpu.VMEM((MLA_HEADS, V_AUG, tq), F32),
                        pltpu.VMEM((MLA_HEADS, tk, tq), BF16), pltpu.VMEM((MLA_HEADS, tk, tq), BF16),
                        pltpu.VMEM((MLA_HEADS, 1, tq), F32), pltpu.VMEM((MLA_HEADS, 1, tq), F32)],
    )
    return pl.pallas_call(
        functools.partial(_flash_kernel, tq=tq, tk=tk, q_offset=q_offset, n_keys=n_keys),
        grid_spec=grid_spec,
        out_shape=jax.ShapeDtypeStruct((batch, HEAD_PAIRS, seq, LANES), BF16),
        compiler_params=_cparams(("arbitrary", "arbitrary")),
        name="flash",
    )(*sched, bounded, q, k, vt)


def _head_block_tables():
    width = MLA_HEADS * HEAD_BLOCK
    seg_q = np.zeros((width, LANES), np.float32)
    seg_k = np.zeros((width, LANES), np.float32)
    place = np.zeros((LANES, width), np.float32)
    for h in range(MLA_HEADS):
        base = h * HEAD_BLOCK
        seg_q[base:base + NOPE_DIM, h] = 1.0
        seg_q[base + NOPE_DIM:base + NOPE_DIM + ROPE_DIM, MLA_HEADS + h] = 1.0
        seg_k[base:base + NOPE_DIM, h] = 1.0
        for d in range(ROPE_DIM):
            place[d, base + NOPE_DIM + d] = 1.0
    cnt_q = np.ones((1, LANES), np.float32)
    cnt_q[0, :MLA_HEADS] = 1.0 / NOPE_DIM
    cnt_q[0, MLA_HEADS:2 * MLA_HEADS] = 1.0 / ROPE_DIM
    cnt_k = np.ones((1, LANES), np.float32)
    cnt_k[0, :MLA_HEADS] = 1.0 / NOPE_DIM
    head_seg = np.zeros((width, LANES), np.float32)
    for h in range(MLA_HEADS):
        head_seg[h * HEAD_BLOCK:(h + 1) * HEAD_BLOCK, h] = 1.0
    return seg_q, seg_k, place, cnt_q, cnt_k, head_seg


def _to_head_blocks(w, per_head, parts):
    k_dim = w.shape[0]
    w3 = w.reshape(k_dim, MLA_HEADS, per_head)
    out = jnp.zeros((k_dim, MLA_HEADS, HEAD_BLOCK), w.dtype)
    for start, size, dest in parts:
        out = out.at[:, :, dest:dest + size].set(w3[:, :, start:start + size])
    return out.reshape(k_dim, MLA_HEADS * HEAD_BLOCK)


def _lane_gain(nope_gain, rope_gain, scale):
    blk = jnp.zeros((HEAD_BLOCK,), F32).at[:NOPE_DIM].set(nope_gain * scale)
    if rope_gain is not None:
        blk = blk.at[NOPE_DIM:NOPE_DIM + ROPE_DIM].set(rope_gain * scale)
    return jnp.tile(blk, MLA_HEADS).reshape(1, MLA_HEADS * HEAD_BLOCK)


def _rope_tables(positions, lane0):
    inv_freq = ROPE_BASE ** (-jnp.arange(ROPE_HALF, dtype=F32) / ROPE_HALF)
    ang = positions.astype(F32)[:, None] * inv_freq[None, :]
    cos, sin = jnp.cos(ang), jnp.sin(ang)
    t = positions.shape[0]
    lead_one = jnp.ones((t, lane0), F32)
    lead = jnp.zeros((t, lane0), F32)
    half = jnp.zeros((t, ROPE_HALF), F32)
    tail = jnp.zeros((t, LANES - lane0 - ROPE_DIM), F32)
    cos_t = jnp.concatenate([lead_one, cos, cos, tail], axis=1)
    sin_a = jnp.concatenate([lead, -sin, half, tail], axis=1)
    sin_b = jnp.concatenate([lead, half, sin, tail], axis=1)
    return cos_t, sin_a, sin_b


def _prep_weights(p):
    w = {}
    n_qkvr = 2 * GLA_DK + 2 * GLA_DV
    w["gla_main"] = p["gla_w_in"][:, :, :n_qkvr].astype(BF16)
    w["gla_a"] = jnp.pad(p["gla_w_in"][:, :, n_qkvr:], ((0, 0), (0, 0), (0, LANES - GLA_GATE_RANK))).astype(BF16)
    w["gla_a2"] = jnp.pad(p["gla_w_alpha2"], ((0, 0), (0, LANES - GLA_GATE_RANK), (0, 0))).astype(BF16)
    w["gla_ba"] = p["gla_b_alpha"].reshape(-1, 1, GLA_DK)
    w["gla_gain"] = p["gla_out_gain"].reshape(-1, 1, GLA_DV_H)
    w["gla_wo"] = p["gla_w_o"].astype(BF16)
    w["mla_wo"] = p["mla_w_o"].astype(BF16)
    w["wr_t"] = p["moe_w_router"].T.astype(BF16)
    w["rbias"] = p["moe_router_bias"].reshape(N_EXPERTS, 1)
    w["wgu"] = p["moe_w_gate_up"].astype(BF16)
    w["wd"] = p["moe_w_down"].astype(BF16)
    w["wsgu"] = p["moe_w_shared_gate_up"].astype(BF16)
    w["wsd"] = p["moe_w_shared_down"].astype(BF16)
    w["kv_wc"] = p["mla_w_dkv"][:, :KV_LORA].astype(BF16)
    w["kv_wr"] = jnp.pad(p["mla_w_dkv"][:, KV_LORA:], ((0, 0), (0, LANES - ROPE_DIM))).astype(BF16)
    w["ckv_gain"] = p["mla_ckv_gain"].reshape(1, KV_LORA)
    w["kr_gain"] = jnp.pad(p["mla_k_rope_gain"], (0, LANES - ROPE_DIM)).reshape(1, LANES)
    per_kv = NOPE_DIM + V_DIM
    w["w_uk"] = _to_head_blocks(p["mla_w_ukv"], per_kv, [(0, NOPE_DIM, 0)]).astype(BF16)
    w_uv = p["mla_w_ukv"].reshape(KV_LORA, MLA_HEADS, per_kv)[:, :, NOPE_DIM:]
    w["w_uv_t"] = w_uv.reshape(KV_LORA, MLA_HEADS * V_DIM).T.astype(BF16)
    per_q = NOPE_DIM + ROPE_DIM
    w["w_dq"] = p["mla_w_dq"].astype(BF16)
    w["w_uq"] = jnp.stack([
        _to_head_blocks(p["mla_w_uq"][j], per_q, [(0, per_q, 0)]) for j in range(p["mla_w_uq"].shape[0])
    ]).astype(BF16)
    scale = (NOPE_DIM + ROPE_DIM) ** -0.5 * LOG2_E
    w["q_lane_gain"] = jnp.stack([
        _lane_gain(p["mla_q_nope_gain"][j], p["mla_q_rope_gain"][j], scale)
        for j in range(p["mla_q_nope_gain"].shape[0])])
    w["k_lane_gain"] = _lane_gain(p["mla_k_nope_gain"], None, 1.0)
    seg_q, seg_k, place, cnt_q, cnt_k, head_seg = _head_block_tables()
    w["head_seg"] = jnp.asarray(head_seg, BF16)
    w["seg_q"] = jnp.asarray(seg_q, BF16)
    w["exp_q"] = jnp.asarray(seg_q.T, BF16)
    w["seg_k"] = jnp.asarray(seg_k, BF16)
    w["exp_k"] = jnp.asarray(seg_k.T, BF16)
    w["place"] = jnp.asarray(place, BF16)
    w["cnt_q"] = jnp.asarray(cnt_q)
    w["cnt_k"] = jnp.asarray(cnt_k)
    return w


def _round_up(a, b):
    return -(-a // b) * b


def _trunk(x, mod_layers, mod_kv, gla_state, cache_ckv, cache_krope, p, w):
    batch, seq, _ = x.shape
    depth = mod_layers.shape[0]
    n_a = p["gla_w_in"].shape[0]
    n = batch * seq
    tm = min(512, seq)
    tpb = seq // tm
    tm_moe = min(MOE_ROW_TILE, seq)
    x2 = x.reshape(n, D_MODEL)
    offset = 0 if cache_ckv is None else cache_ckv.shape[1]
    q_pos = offset + jnp.arange(seq, dtype=jnp.int32)
    kv_tabs = _rope_tables(q_pos, 0)
    q_tabs = _rope_tables(q_pos, NOPE_DIM)
    new_states = []
    ckv_new = krope_new = k_all = vt_all = k_n2 = None
    n_keys = 0
    tq = min(512, _round_up(seq, LANES))
    seq_q = _round_up(seq, tq)
    tk = 512
    for layer in range(depth):
        mod = mod_layers[layer].reshape(batch * 6, 1, D_MODEL)
        gain_m = p["norm_mix_gain"][layer].reshape(1, D_MODEL)
        gain_f = p["norm_ffn_gain"][layer].reshape(1, D_MODEL)
        if layer < n_a:
            q, k, v, r, la = _gla_pre(x2, mod, 6, 0, tm, tpb, gain_m, w["gla_main"][layer],
                                      w["gla_a"][layer], w["gla_a2"][layer], w["gla_ba"][layer])
            chunk = min(CHUNK, seq)
            rows_per_step = min(512, seq)
            s0_t = jnp.swapaxes(gla_state[layer], -1, -2)
            y, s_t = _gla_core(q, k, la, v, r, w["gla_gain"][layer], s0_t, batch, seq, chunk, rows_per_step)
            new_states.append(jnp.swapaxes(s_t, -1, -2))
            w_o = w["gla_wo"][layer]
        else:
            j = layer - n_a
            qh, q_n2 = _mla_pre(x2, mod, 6, 0, tm, tpb, batch, seq, gain_m, w["w_dq"][j],
                                p["mla_q_a_gain"][j].reshape(1, Q_LORA), w["w_uq"][j], w["seg_q"],
                                w["exp_q"], w["cnt_q"], w["q_lane_gain"][j], w["head_seg"], *q_tabs)
            if seq_q != seq:
                qh = jnp.pad(qh, ((0, 0), (0, 0), (0, 0), (0, seq_q - seq)))
            bounded = (jnp.max(q_n2 * k_n2, axis=(1, 2)) * NORM_SLACK <= LOGIT_LIMIT ** 2).astype(jnp.int32)
            y = _flash(qh, k_all, vt_all, bounded, tq, tk, offset, n_keys)
            if seq_q != seq:
                y = y[:, :, :seq, :]
            w_o = w["mla_wo"][j]
        if n <= MOE_ROW_TILE and batch > 1:
            post_mod, post_tm, post_tpb = jnp.repeat(mod_layers[layer], seq, axis=0), n, 0
        else:
            post_mod, post_tm, post_tpb = mod, tm_moe, seq // tm_moe
        x2 = _post(x2, y, post_mod, 6, 0, post_tm, post_tpb, w_o, gain_f, w["wr_t"], w["rbias"],
                   w["wgu"][layer], w["wd"][layer], w["wsgu"][layer], w["wsd"][layer])
        if layer == n_a - 1:
            mkv = mod_kv.reshape(batch * 2, 1, D_MODEL)
            ckv2, kr2 = _kv_latent(x2, mkv, tm, tpb, p["kv_norm_gain"].reshape(1, D_MODEL), w["kv_wc"],
                                   w["kv_wr"], w["ckv_gain"], w["kr_gain"], *kv_tabs)
            ckv_new = ckv2.reshape(batch, seq, KV_LORA)
            kr_pad = kr2.reshape(batch, seq, LANES)
            krope_new = kr_pad[:, :, :ROPE_DIM]
            if cache_ckv is None:
                ckv_all, kr_all = ckv_new, kr_pad
            else:
                ckv_all = jnp.concatenate([cache_ckv, ckv_new], axis=1)
                kr_all = jnp.concatenate(
                    [jnp.pad(cache_krope, ((0, 0), (0, 0), (0, LANES - ROPE_DIM))), kr_pad], axis=1)
            n_keys = ckv_all.shape[1]
            lp = _round_up(n_keys, tk)
            if lp != n_keys:
                ckv_all = jnp.pad(ckv_all, ((0, 0), (0, lp - n_keys), (0, 0)))
                kr_all = jnp.pad(kr_all, ((0, 0), (0, lp - n_keys), (0, 0)))
            k_all, vt_all, k_n2 = _kv_expand(ckv_all, kr_all, tk, w["w_uk"], w["w_uv_t"], w["seg_k"], w["exp_k"],
                                       w["cnt_k"], w["k_lane_gain"], w["place"], w["head_seg"])
    return x2.reshape(batch, seq, D_MODEL), jnp.stack(new_states), ckv_new, krope_new


def kernel(x_prompt, x_sample, c_prompt, c_sample, state_gla, cache_kv_latent, cache_k_rope, w_ada, b_ada, norm_mix_gain, norm_ffn_gain, gla_w_in, gla_w_alpha2, gla_b_alpha, gla_out_gain, gla_w_o, kv_w_ada, kv_b_ada, kv_norm_gain, mla_w_dkv, mla_ckv_gain, mla_k_rope_gain, mla_w_ukv, mla_k_nope_gain, mla_w_dq, mla_q_a_gain, mla_w_uq, mla_q_nope_gain, mla_q_rope_gain, mla_w_o, moe_w_router, moe_router_bias, moe_w_gate_up, moe_w_down, moe_w_shared_gate_up, moe_w_shared_down):
    p = dict(norm_mix_gain=norm_mix_gain, norm_ffn_gain=norm_ffn_gain, gla_w_in=gla_w_in,
             gla_w_alpha2=gla_w_alpha2, gla_b_alpha=gla_b_alpha, gla_out_gain=gla_out_gain,
             gla_w_o=gla_w_o, kv_norm_gain=kv_norm_gain, mla_w_dkv=mla_w_dkv, mla_ckv_gain=mla_ckv_gain,
             mla_k_rope_gain=mla_k_rope_gain, mla_w_ukv=mla_w_ukv, mla_k_nope_gain=mla_k_nope_gain,
             mla_w_dq=mla_w_dq, mla_q_a_gain=mla_q_a_gain, mla_w_uq=mla_w_uq,
             mla_q_nope_gain=mla_q_nope_gain, mla_q_rope_gain=mla_q_rope_gain, mla_w_o=mla_w_o,
             moe_w_router=moe_w_router, moe_router_bias=moe_router_bias, moe_w_gate_up=moe_w_gate_up,
             moe_w_down=moe_w_down, moe_w_shared_gate_up=moe_w_shared_gate_up,
             moe_w_shared_down=moe_w_shared_down)
    w = _prep_weights(p)
    depth = w_ada.shape[0]
    bp, bs = c_prompt.shape[0], c_sample.shape[0]
    rows = _round_up(bp + bs, 8)
    c_all = jnp.concatenate([c_prompt, c_sample, jnp.zeros((rows - bp - bs, D_MODEL), F32)], axis=0)
    mod_l = _modulation(c_all, w_ada, b_ada.reshape(depth, 1, 6 * D_MODEL))
    mod_k = _modulation(c_all, kv_w_ada.reshape(1, D_MODEL, 2 * D_MODEL),
                        kv_b_ada.reshape(1, 1, 2 * D_MODEL))[0]
    gla_zero = jnp.zeros((state_gla.shape[0], bp) + state_gla.shape[2:], x_prompt.dtype)
    y_p, s_p, ckv_p, kr_p = _trunk(x_prompt, mod_l[:, :bp], mod_k[:bp], gla_zero, None, None, p, w)
    y_s, s_s, ckv_s, kr_s = _trunk(x_sample, mod_l[:, bp:bp + bs], mod_k[bp:bp + bs], state_gla,
                                   cache_kv_latent, cache_k_rope, p, w)
    return (y_p, y_s, s_p, ckv_p, kr_p, s_s, ckv_s, kr_s)
```
